```python
import math
import jax
import jax.numpy as jnp
from jax import lax
import numpy as np

D_MODEL = 4096
BATCH = 4
SEQ = 2048
DEPTH = 2
DEC_BATCH = 8
DEC_SEQ = 8
PAST_LEN = 16384
PAGE_SIZE = 128

HEAD_DIM = 128
D_MIX = D_MODEL
H_ATT = (3 * D_MIX) // (8 * HEAD_DIM)
DK_HALF = HEAD_DIM // 2
W_ATT = H_ATT * HEAD_DIM
H_RET = H_ATT
DK_RET = HEAD_DIM
DV_RET = HEAD_DIM
W_RET = H_RET * DV_RET
W_SSM = D_MIX - W_ATT - W_RET
SSM_H = 16
SSM_G = W_SSM // SSM_H
SSM_P = 64
N_IN = 3 * W_ATT + 4 * W_RET + W_SSM
D_FF = ((8 * D_MODEL + 3 * 256 - 1) // (3 * 256)) * 256
ROPE_THETA = 10000.0
Q_BLOCK = 128
RET_CHUNK = 128
EPS = 1e-6

kernel_name = 'hybrid_diffattn_retention_s5_step'


def rmsnorm(x, g=None):
    xf = x.astype(jnp.float32)
    y = xf * lax.rsqrt(jnp.mean(xf * xf, axis=-1, keepdims=True) + EPS)
    if g is not None:
        y = y * g.astype(jnp.float32)
    return y.astype(x.dtype)


def att_freqs():
    return ROPE_THETA ** (-jnp.arange(0, DK_HALF, 2, dtype=jnp.float32) / DK_HALF)


def ret_freqs():
    return 1.0 / (ROPE_THETA ** jnp.linspace(0.0, 1.0, DK_RET // 2, dtype=jnp.float32))


def rope(x, pos, freqs):
    ang = pos.astype(jnp.float32)[:, None] * freqs[None, :]
    shape = (1, pos.shape[0]) + (1,) * (x.ndim - 3) + (freqs.shape[0],)
    cos = jnp.cos(ang).reshape(shape)
    sin = jnp.sin(ang).reshape(shape)
    x1, x2 = jnp.split(x.astype(jnp.float32), 2, axis=-1)
    return jnp.concatenate([x1 * cos - x2 * sin, x2 * cos + x1 * sin], axis=-1).astype(x.dtype)


def diff_attn_core(q, k, v, lam, q_pos, k_pos):
    kk = k.reshape(k.shape[:3] + (2, DK_HALF))
    s = jnp.einsum('bqhmd,bkhmd->bmhqk', q, kk).astype(jnp.float32) * (DK_HALF ** -0.5)
    mask = k_pos[None, :] <= q_pos[:, None]
    s = jnp.where(mask, s, -jnp.inf)
    p = jax.nn.softmax(s, axis=-1)
    w = p[:, 0] - lam * p[:, 1]
    return jnp.einsum('bhqk,bkhe->bqhe', w.astype(v.dtype), v)


def prompt_attend(q, k, v, lam):
    b, l, h = q.shape[0], q.shape[1], q.shape[2]
    nb = l // Q_BLOCK
    qb = q.reshape(b, nb, Q_BLOCK, h, 2, DK_HALF).swapaxes(0, 1)
    pos = jnp.arange(l, dtype=jnp.int32)
    qpb = pos.reshape(nb, Q_BLOCK)
    ob = lax.map(lambda a: diff_attn_core(a[0], k, v, lam, a[1], pos), (qb, qpb))
    return ob.swapaxes(0, 1).reshape(b, l, h, HEAD_DIM)


def make_sample_attend(k_past, v_past):
    def attend(q, k, v, lam):
        past = k_past.shape[1]
        n_new = q.shape[1]
        k_all = jnp.concatenate([k_past.astype(k.dtype), k], axis=1)
        v_all = jnp.concatenate([v_past.astype(v.dtype), v], axis=1)
        k_pos = jnp.arange(past + n_new, dtype=jnp.int32)
        q_pos = past + jnp.arange(n_new, dtype=jnp.int32)
        return diff_attn_core(q, k_all, v_all, lam, q_pos, k_pos)
    return attend


def ret_log_gamma():
    return jnp.log(1.0 - 2.0 ** (-5.0 - jnp.arange(H_RET, dtype=jnp.float32)))


def retention_chunk(s_prev, qkv):
    q, k, v = qkv
    c = q.shape[1]
    log_g = ret_log_gamma()
    idx = jnp.arange(c, dtype=jnp.float32)
    diff = idx[:, None] - idx[None, :]
    dec = jnp.where(diff >= 0, jnp.exp(jnp.maximum(diff, 0.0)[None] * log_g[:, None, None]), 0.0)
    s = jnp.einsum('bjhd,bihd->bhji', q, k) * dec[None]
    o = jnp.einsum('bhji,bihe->bjhe', s, v)
    cross = jnp.exp((idx + 1.0)[:, None] * log_g[None, :])[None, :, :, None]
    o = o + jnp.einsum('bjhd,bhde->bjhe', q, s_prev) * cross
    k_dec = k * jnp.exp((c - 1.0 - idx)[:, None] * log_g[None, :])[None, :, :, None]
    s_new = s_prev * jnp.exp(c * log_g)[None, :, None, None] + jnp.einsum('bihd,bihe->bhde', k_dec, v)
    return s_new, o


def retention(q, k, v, s0):
    b, l = q.shape[0], q.shape[1]
    c = RET_CHUNK if l % RET_CHUNK == 0 else l
    n = l // c

    def to_chunks(t):
        return t.astype(jnp.float32).reshape((b, n, c) + t.shape[2:]).swapaxes(0, 1)

    s_fin, o = lax.scan(retention_chunk, s0.astype(jnp.float32), (to_chunks(q), to_chunks(k), to_chunks(v)))
    return o.swapaxes(0, 1).reshape(b, l, H_RET, DV_RET), s_fin


def _complex_affine_combine(e1, e2):
    a1r, a1i, b1r, b1i = e1
    a2r, a2i, b2r, b2i = e2
    return (a2r * a1r - a2i * a1i,
            a2r * a1i + a2i * a1r,
            a2r * b1r - a2i * b1i + b2r,
            a2r * b1i + a2i * b1r + b2i)


def s5_scan(u, h0_re, h0_im, lam_re, lam_im, log_dt, b_re, b_im, c_re, c_im, d):
    f32 = jnp.float32
    lam_re, lam_im = lam_re.astype(f32), lam_im.astype(f32)
    dt = jnp.exp(log_dt.astype(f32))[:, None]
    mag = jnp.exp(lam_re * dt)
    ang = lam_im * dt
    a_re, a_im = mag * jnp.cos(ang), mag * jnp.sin(ang)
    den = lam_re * lam_re + lam_im * lam_im
    num_re = a_re - 1.0
    coef_re = (num_re * lam_re + a_im * lam_im) / den
    coef_im = (a_im * lam_re - num_re * lam_im) / den
    b_re, b_im = b_re.astype(f32), b_im.astype(f32)
    bb_re = coef_re[..., None] * b_re - coef_im[..., None] * b_im
    bb_im = coef_re[..., None] * b_im + coef_im[..., None] * b_re
    bu_re = jnp.einsum('blgh,gph->blgp', u, bb_re)
    bu_im = jnp.einsum('blgh,gph->blgp', u, bb_im)
    bu_re = bu_re.at[:, 0].add(a_re * h0_re - a_im * h0_im)
    bu_im = bu_im.at[:, 0].add(a_re * h0_im + a_im * h0_re)
    big_a_re = jnp.broadcast_to(a_re, bu_re.shape)
    big_a_im = jnp.broadcast_to(a_im, bu_im.shape)
    _, _, x_re, x_im = lax.associative_scan(_complex_affine_combine, (big_a_re, big_a_im, bu_re, bu_im), axis=1)
    y = (jnp.einsum('blgp,ghp->blgh', x_re, c_re.astype(f32))
         - jnp.einsum('blgp,ghp->blgh', x_im, c_im.astype(f32))
         + d.astype(f32) * u)
    return y, x_re[:, -1], x_im[:, -1]


def hybrid_layer(x, pos, attend, ret_s0, s5_h0_re, s5_h0_im, lp, lam_init):
    f32 = jnp.float32
    b, l = x.shape[0], x.shape[1]
    h = rmsnorm(x, lp['norm_mix_g'])
    z = h @ lp['w_in']
    sizes = (W_ATT, W_ATT, W_ATT, W_RET, W_RET, W_RET, W_RET, W_SSM)
    cuts = [sum(sizes[:i + 1]) for i in range(len(sizes) - 1)]
    qa, ka, va, qr, kr, vr, gr, us = jnp.split(z, cuts, axis=-1)

    qa = rope(rmsnorm(qa.reshape(b, l, H_ATT, 2, DK_HALF), lp['q_norm_g']), pos, att_freqs())
    ka = rope(rmsnorm(ka.reshape(b, l, H_ATT, 2, DK_HALF), lp['k_norm_g']), pos, att_freqs())
    ka = ka.reshape(b, l, H_ATT, HEAD_DIM)
    va = va.reshape(b, l, H_ATT, HEAD_DIM)
    lam = (jnp.exp(jnp.sum(lp['lambda_q1'].astype(f32) * lp['lambda_k1'].astype(f32)))
           - jnp.exp(jnp.sum(lp['lambda_q2'].astype(f32) * lp['lambda_k2'].astype(f32))) + lam_init)
    o_att = attend(qa, ka, va, lam)
    o_att = (rmsnorm(o_att, lp['subln_g']) * (1.0 - lam_init)).reshape(b, l, W_ATT)

    qr = rope(qr.reshape(b, l, H_RET, DK_RET), pos, ret_freqs())
    kr = rope(kr.reshape(b, l, H_RET, DK_RET), pos, ret_freqs()) * (DK_RET ** -0.5)
    vr = vr.reshape(b, l, H_RET, DV_RET)
    o_ret, s_ret = retention(qr, kr, vr, ret_s0)
    o_ret = (rmsnorm(o_ret) * jax.nn.silu(gr.astype(f32)).reshape(b, l, H_RET, DV_RET)).reshape(b, l, W_RET)

    u = us.astype(f32).reshape(b, l, SSM_G, SSM_H)
    y, s5_re, s5_im = s5_scan(u, s5_h0_re.astype(f32), s5_h0_im.astype(f32), lp['s5_lambda_re'], lp['s5_lambda_im'],
                              lp['s5_log_dt'], lp['s5_b_re'], lp['s5_b_im'], lp['s5_c_re'], lp['s5_c_im'], lp['s5_d'])
    y = jax.nn.gelu(y.reshape(b, l, W_SSM))
    y = y * jax.nn.sigmoid(y @ lp['s5_w_glu'].astype(f32))
    o_ssm = rmsnorm(y, lp['s5_norm_g'])

    mix = jnp.concatenate([o_att.astype(x.dtype), o_ret.astype(x.dtype), o_ssm.astype(x.dtype)], axis=-1)
    x = x + mix @ lp['w_out']
    h = rmsnorm(x, lp['norm_ffn_g'])
    x = x + (jax.nn.silu(h @ lp['w_gate']) * (h @ lp['w_up'])) @ lp['w_down']
    return x, ka, va, s_ret, s5_re, s5_im


def setup_inputs(seed: int = 0) -> dict:
    key = jax.random.key(seed)
    ks = jax.random.split(key, 32)
    f32 = jnp.float32
    n_pages = PAST_LEN // PAGE_SIZE
    n_used = DEC_BATCH * n_pages
    n_pool = n_used + max(1, n_used // 4)

    def nrm(k, shape, s):
        return jax.random.normal(k, shape, f32) * s

    def gain(k, shape):
        return 1.0 + 0.01 * jax.random.normal(k, shape, f32)

    page_table = jax.random.permutation(ks[4], n_pool)[:n_used].astype(jnp.int32).reshape(DEC_BATCH, n_pages)
    return {
        'x_prompt': nrm(ks[0], (BATCH, SEQ, D_MODEL), 1.0),
        'x_sample': nrm(ks[1], (DEC_BATCH, DEC_SEQ, D_MODEL), 1.0),
        'cache_k': nrm(ks[2], (DEPTH, n_pool, PAGE_SIZE, H_ATT, HEAD_DIM), 1.0),
        'cache_v': nrm(ks[3], (DEPTH, n_pool, PAGE_SIZE, H_ATT, HEAD_DIM), 1.0),
        'page_table': page_table,
        'state_ret': nrm(ks[5], (DEPTH, DEC_BATCH, H_RET, DK_RET, DV_RET), 1.0),
        'state_s5_re': nrm(ks[6], (DEPTH, DEC_BATCH, SSM_G, SSM_P), 0.3),
        'state_s5_im': nrm(ks[7], (DEPTH, DEC_BATCH, SSM_G, SSM_P), 0.3),
        'norm_mix_g': gain(ks[8], (DEPTH, D_MODEL)),
        'w_in': nrm(ks[9], (DEPTH, D_MODEL, N_IN), D_MODEL ** -0.5),
        'q_norm_g': gain(ks[10], (DEPTH, DK_HALF)),
        'k_norm_g': gain(ks[11], (DEPTH, DK_HALF)),
        'lambda_q1': nrm(ks[12], (DEPTH, DK_HALF), 0.1),
        'lambda_k1': nrm(ks[13], (DEPTH, DK_HALF), 0.1),
        'lambda_q2': nrm(ks[14], (DEPTH, DK_HALF), 0.1),
        'lambda_k2': nrm(ks[15], (DEPTH, DK_HALF), 0.1),
        'subln_g': gain(ks[16], (DEPTH, HEAD_DIM)),
        's5_lambda_re': -0.5 + nrm(ks[17], (DEPTH, SSM_G, SSM_P), 0.01),
        's5_lambda_im': jnp.pi * jnp.arange(SSM_P, dtype=f32)[None, None, :] + nrm(ks[18], (DEPTH, SSM_G, SSM_P), 0.01),
        's5_log_dt': jax.random.uniform(ks[19], (DEPTH, SSM_G), f32, math.log(1e-3), math.log(1e-1)),
        's5_b_re': nrm(ks[20], (DEPTH, SSM_G, SSM_P, SSM_H), (2 * SSM_H) ** -0.5),
        's5_b_im': nrm(ks[21], (DEPTH, SSM_G, SSM_P, SSM_H), (2 * SSM_H) ** -0.5),
        's5_c_re': nrm(ks[22], (DEPTH, SSM_G, SSM_H, SSM_P), SSM_P ** -0.5),
        's5_c_im': nrm(ks[23], (DEPTH, SSM_G, SSM_H, SSM_P), SSM_P ** -0.5),
        's5_d': nrm(ks[24], (DEPTH, SSM_G, SSM_H), 1.0),
        's5_w_glu': nrm(ks[25], (DEPTH, W_SSM, W_SSM), W_SSM ** -0.5),
        's5_norm_g': gain(ks[26], (DEPTH, W_SSM)),
        'w_out': nrm(ks[27], (DEPTH, D_MIX, D_MODEL), D_MIX ** -0.5),
        'norm_ffn_g': gain(ks[28], (DEPTH, D_MODEL)),
        'w_gate': nrm(ks[29], (DEPTH, D_MODEL, D_FF), D_MODEL ** -0.5),
        'w_up': nrm(ks[30], (DEPTH, D_MODEL, D_FF), D_MODEL ** -0.5),
        'w_down': nrm(ks[31], (DEPTH, D_FF, D_MODEL), D_FF ** -0.5),
    }


def reference(x_prompt, x_sample, cache_k, cache_v, page_table, state_ret, state_s5_re, state_s5_im,
              norm_mix_g, w_in, q_norm_g, k_norm_g, lambda_q1, lambda_k1, lambda_q2, lambda_k2, subln_g,
              s5_lambda_re, s5_lambda_im, s5_log_dt, s5_b_re, s5_b_im, s5_c_re, s5_c_im, s5_d, s5_w_glu,
              s5_norm_g, w_out, norm_ffn_g, w_gate, w_up, w_down):
    f32 = jnp.float32
    b_p, seq = x_prompt.shape[0], x_prompt.shape[1]
    b_s, n_new = x_sample.shape[0], x_sample.shape[1]
    past = page_table.shape[1] * PAGE_SIZE
    pos_p = jnp.arange(seq, dtype=jnp.int32)
    pos_s = past + jnp.arange(n_new, dtype=jnp.int32)
    yp, ys = x_prompt, x_sample
    kp_l, vp_l, ks_l, vs_l, rp_l, rs_l, pre_l, pim_l, sre_l, sim_l = ([] for _ in range(10))
    for l in range(DEPTH):
        lp = {
            'norm_mix_g': norm_mix_g[l], 'w_in': w_in[l], 'q_norm_g': q_norm_g[l], 'k_norm_g': k_norm_g[l],
            'lambda_q1': lambda_q1[l], 'lambda_k1': lambda_k1[l], 'lambda_q2': lambda_q2[l],
            'lambda_k2': lambda_k2[l], 'subln_g': subln_g[l],
            's5_lambda_re': s5_lambda_re[l], 's5_lambda_im': s5_lambda_im[l], 's5_log_dt': s5_log_dt[l],
            's5_b_re': s5_b_re[l], 's5_b_im': s5_b_im[l], 's5_c_re': s5_c_re[l], 's5_c_im': s5_c_im[l],
            's5_d': s5_d[l], 's5_w_glu': s5_w_glu[l], 's5_norm_g': s5_norm_g[l],
            'w_out': w_out[l], 'norm_ffn_g': norm_ffn_g[l],
            'w_gate': w_gate[l], 'w_up': w_up[l], 'w_down': w_down[l],
        }
        lam_init = 0.8 - 0.6 * math.exp(-0.3 * l)
        yp, kp, vp, rp, p_re, p_im = hybrid_layer(
            yp, pos_p, prompt_attend,
            jnp.zeros((b_p, H_RET, DK_RET, DV_RET), f32),
            jnp.zeros((b_p, SSM_G, SSM_P), f32), jnp.zeros((b_p, SSM_G, SSM_P), f32), lp, lam_init)
        k_past = cache_k[l, page_table].reshape(b_s, past, H_ATT, HEAD_DIM)
        v_past = cache_v[l, page_table].reshape(b_s, past, H_ATT, HEAD_DIM)
        ys, k_s, v_s, r_s, s_re, s_im = hybrid_layer(
            ys, pos_s, make_sample_attend(k_past, v_past),
            state_ret[l], state_s5_re[l], state_s5_im[l], lp, lam_init)
        kp_l.append(kp); vp_l.append(vp); ks_l.append(k_s); vs_l.append(v_s)
        rp_l.append(rp); rs_l.append(r_s); pre_l.append(p_re); pim_l.append(p_im)
        sre_l.append(s_re); sim_l.append(s_im)
    return (yp, ys, jnp.stack(kp_l), jnp.stack(vp_l), jnp.stack(ks_l), jnp.stack(vs_l),
            jnp.stack(rp_l), jnp.stack(rs_l), jnp.stack(pre_l), jnp.stack(pim_l),
            jnp.stack(sre_l), jnp.stack(sim_l))
```

```python
import functools
import math

import jax
import jax.numpy as jnp
from jax import lax
from jax.experimental import pallas as pl
from jax.experimental.pallas import tpu as pltpu

F32 = jnp.float32
BF16 = jnp.bfloat16

HEAD_DIM = 128
DK_HALF = HEAD_DIM // 2
SSM_H = 16
SSM_P = 64
PAGE_SIZE = 128
RET_CHUNK = 128
ROPE_THETA = 10000.0
EPS = 1e-6
LANES = 128
SUBLANES = 8
S5_GROUPS_PER_BLOCK = LANES // SSM_H
S5_STATE_COLS = S5_GROUPS_PER_BLOCK * SSM_P
NEG_BIG = -1e30
V7X_VMEM_LIMIT_BYTES = 56 * 1024 * 1024


def _pick_tile(n, candidates):
    for c in candidates:
        if n % c == 0:
            return c
    return n


def _params(semantics):
    return pltpu.CompilerParams(dimension_semantics=semantics, vmem_limit_bytes=V7X_VMEM_LIMIT_BYTES)


def _sigmoid(x):
    return 1.0 / (1.0 + jnp.exp(-x))


def _rmsnorm_kernel(x_ref, g_ref, o_ref):
    x = x_ref[...]
    ms = jnp.mean(x * x, axis=-1, keepdims=True)
    o_ref[...] = (x * lax.rsqrt(ms + EPS) * g_ref[...]).astype(o_ref.dtype)


def _rmsnorm(x, g):
    m, d = x.shape
    tm = _pick_tile(m, (512, 256, 128, 64))
    return pl.pallas_call(
        _rmsnorm_kernel,
        out_shape=jax.ShapeDtypeStruct((m, d), BF16),
        grid=(m // tm,),
        in_specs=[pl.BlockSpec((tm, d), lambda i: (i, 0)), pl.BlockSpec((1, d), lambda i: (0, 0))],
        out_specs=pl.BlockSpec((tm, d), lambda i: (i, 0)),
        compiler_params=_params(("parallel",)),
        name="rmsnorm",
    )(x, g.reshape(1, d))


def _mm_kernel(*refs, n_a, mode):
    a_refs = refs[:n_a]
    o_ref = refs[-1]
    if mode == "swiglu":
        bg_ref, bu_ref = refs[n_a], refs[n_a + 1]
        a = a_refs[0][...]
        g = jnp.dot(a, bg_ref[...], preferred_element_type=F32)
        u = jnp.dot(a, bu_ref[...], preferred_element_type=F32)
        o_ref[...] = (g * _sigmoid(g) * u).astype(o_ref.dtype)
        return
    b_refs = refs[n_a:2 * n_a]
    acc = jnp.dot(a_refs[0][...], b_refs[0][...], preferred_element_type=F32)
    for a_ref, b_ref in zip(a_refs[1:], b_refs[1:]):
        acc = acc + jnp.dot(a_ref[...], b_ref[...], preferred_element_type=F32)
    if mode == "res":
        acc = refs[2 * n_a][...] + acc
    o_ref[...] = acc.astype(o_ref.dtype)


def _matmul(a_list, b, row_blocks, *, mode="plain", res=None, b2=None, out_dtype=F32, tm, tn, name):
    m = a_list[0].shape[0]
    n = b.shape[1]
    tm = _pick_tile(m, (tm, 512, 256, 128, 64))
    tn = _pick_tile(n, (tn, 512, 256, 128))
    in_specs = [pl.BlockSpec((tm, a.shape[1]), lambda i, j: (i, 0)) for a in a_list]
    args = list(a_list)
    for a, rb in zip(a_list, row_blocks):
        in_specs.append(pl.BlockSpec((a.shape[1], tn), functools.partial(lambda i, j, rb: (rb, j), rb=rb)))
        args.append(b)
    if mode == "swiglu":
        in_specs.append(pl.BlockSpec((a_list[0].shape[1], tn), lambda i, j: (0, j)))
        args.append(b2)
    if mode == "res":
        in_specs.append(pl.BlockSpec((tm, tn), lambda i, j: (i, j)))
        args.append(res)
    return pl.pallas_call(
        functools.partial(_mm_kernel, n_a=len(a_list), mode=mode),
        out_shape=jax.ShapeDtypeStruct((m, n), out_dtype),
        grid=(m // tm, n // tn),
        in_specs=in_specs,
        out_specs=pl.BlockSpec((tm, tn), lambda i, j: (i, j)),
        compiler_params=_params(("parallel", "arbitrary")),
        name=name,
    )(*args)


def _half_rmsnorm(x, g):
    lane = lax.broadcasted_iota(jnp.int32, x.shape, 1)
    first = lane < DK_HALF
    sq = x * x
    tot = jnp.sum(sq, axis=-1, keepdims=True)
    s1 = jnp.sum(jnp.where(first, sq, 0.0), axis=-1, keepdims=True)
    ms = jnp.where(first, s1, tot - s1) * (1.0 / DK_HALF)
    return x * lax.rsqrt(ms + EPS) * g


def _att_rope(x, cos, sin_a, sin_b):
    return x * cos + pltpu.roll(x, LANES - DK_HALF // 2, 1) * sin_a + pltpu.roll(x, DK_HALF // 2, 1) * sin_b


def _ret_rope(x, cos, sin_signed):
    return x * cos + pltpu.roll(x, DK_HALF, 1) * sin_signed


def _stack_half_masked(q):
    lane = lax.broadcasted_iota(jnp.int32, q.shape, 1)
    first = lane < DK_HALF
    return jnp.concatenate([jnp.where(first, q, 0.0), jnp.where(first, 0.0, q)], axis=0)


def _softmax_step(s, v, m_ref, l_ref, acc_ref):
    m_prev = m_ref[...]
    m_new = jnp.maximum(m_prev, jnp.max(s, axis=-1, keepdims=True))
    alpha = jnp.exp(m_prev - m_new)
    p = jnp.exp(s - m_new)
    l_ref[...] = alpha * l_ref[...] + jnp.sum(p, axis=-1, keepdims=True)
    acc_ref[...] = alpha * acc_ref[...] + jnp.dot(p.astype(BF16), v, preferred_element_type=F32)
    m_ref[...] = m_new


def _sub_layer_norm(o, g, out_scale):
    ms = jnp.mean(o * o, axis=-1, keepdims=True)
    return o * lax.rsqrt(ms + EPS) * g * out_scale


def _prompt_attn_kernel(zq_ref, zk_ref, zv_ref, cq_ref, saq_ref, sbq_ref, ck_ref, sak_ref, sbk_ref,
                        qg_ref, kg_ref, sg_ref, lam_ref,
                        o_ref, newk_ref, newv_ref,
                        k_s, v_s, q2_s, m_s, l_s, acc_s, *, tq, out_scale):
    i = pl.program_id(2)

    @pl.when(i == 0)
    def _():
        k = _att_rope(_half_rmsnorm(zk_ref[...], kg_ref[...]), ck_ref[...], sak_ref[...], sbk_ref[...])
        newk_ref[...] = k
        k_s[...] = k.astype(BF16)
        v = zv_ref[...]
        newv_ref[...] = v
        v_s[...] = v.astype(BF16)

    q = _att_rope(_half_rmsnorm(zq_ref[...], qg_ref[...]), cq_ref[...], saq_ref[...], sbq_ref[...])
    q2_s[...] = _stack_half_masked(q * (DK_HALF ** -0.5)).astype(BF16)
    m_s[...] = jnp.full(m_s.shape, NEG_BIG, F32)
    l_s[...] = jnp.zeros(l_s.shape, F32)
    acc_s[...] = jnp.zeros(acc_s.shape, F32)

    def scores(j):
        r0 = pl.multiple_of(j * tq, tq)
        kj = k_s[pl.ds(r0, tq), :]
        vj = v_s[pl.ds(r0, tq), :]
        s = lax.dot_general(q2_s[...], kj, (((1,), (1,)), ((), ())), preferred_element_type=F32)
        return s, vj

    def body(j, carry):
        s, vj = scores(j)
        _softmax_step(s, vj, m_s, l_s, acc_s)
        return carry

    lax.fori_loop(0, i, body, 0)
    s, vj = scores(i)
    row = lax.broadcasted_iota(jnp.int32, s.shape, 0)
    row = jnp.where(row >= tq, row - tq, row)
    col = lax.broadcasted_iota(jnp.int32, s.shape, 1)
    _softmax_step(jnp.where(col <= row, s, NEG_BIG), vj, m_s, l_s, acc_s)

    o_full = acc_s[...] / l_s[...]
    o = o_full[:tq] - lam_ref[...] * o_full[tq:]
    o_ref[...] = _sub_layer_norm(o, sg_ref[...], out_scale).astype(o_ref.dtype)


def _prompt_attention(z, b, l, w_att, tabs, qg, kg, sg, lam_row, out_scale):
    m = z.shape[0]
    h_att = w_att // HEAD_DIM
    tq = _pick_tile(l, (512, 256, 128))
    nq = l // tq
    cos, sa, sb = tabs
    q_spec = pl.BlockSpec((tq, HEAD_DIM), lambda bb, h, i: (bb * nq + i, h))
    k_spec = pl.BlockSpec((l, HEAD_DIM), lambda bb, h, i: (bb, h_att + h))
    v_spec = pl.BlockSpec((l, HEAD_DIM), lambda bb, h, i: (bb, 2 * h_att + h))
    tq_spec = pl.BlockSpec((tq, HEAD_DIM), lambda bb, h, i: (i, 0))
    tk_spec = pl.BlockSpec((l, HEAD_DIM), lambda bb, h, i: (0, 0))
    row_spec = pl.BlockSpec((1, HEAD_DIM), lambda bb, h, i: (0, 0))
    kv_out_spec = pl.BlockSpec((l, HEAD_DIM), lambda bb, h, i: (bb, h))
    return pl.pallas_call(
        functools.partial(_prompt_attn_kernel, tq=tq, out_scale=out_scale),
        out_shape=(jax.ShapeDtypeStruct((m, w_att), BF16),
                   jax.ShapeDtypeStruct((m, w_att), F32),
                   jax.ShapeDtypeStruct((m, w_att), F32)),
        grid=(b, h_att, nq),
        in_specs=[q_spec, k_spec, v_spec, tq_spec, tq_spec, tq_spec, tk_spec, tk_spec, tk_spec,
                  row_spec, row_spec, row_spec, row_spec],
        out_specs=(q_spec, kv_out_spec, kv_out_spec),
        scratch_shapes=[pltpu.VMEM((l, HEAD_DIM), BF16), pltpu.VMEM((l, HEAD_DIM), BF16),
                        pltpu.VMEM((2 * tq, HEAD_DIM), BF16),
                        pltpu.VMEM((2 * tq, 1), F32), pltpu.VMEM((2 * tq, 1), F32),
                        pltpu.VMEM((2 * tq, HEAD_DIM), F32)],
        compiler_params=_params(("parallel", "parallel", "arbitrary")),
        name="prompt_diff_attention",
    )(z, z, z, cos, sa, sb, cos, sa, sb, qg, kg, sg, lam_row)


def _sample_attn_kernel(pt_ref, zq_ref, zk_ref, zv_ref, kc_ref, vc_ref, cos_ref, sa_ref, sb_ref,
                        qg_ref, kg_ref, sg_ref, lam_ref,
                        o_ref, newk_ref, newv_ref,
                        q2_s, kn_s, vn_s, m_s, l_s, acc_s, *, h_att, n_new, out_scale):
    del pt_ref
    p = pl.program_id(1)
    rows_h = 2 * n_new

    @pl.when(p == 0)
    def _():
        kn_s[...] = jnp.zeros(kn_s.shape, F32)
        vn_s[...] = jnp.zeros(vn_s.shape, F32)
        for h in range(h_att):
            cs = slice(h * HEAD_DIM, (h + 1) * HEAD_DIM)
            q = _att_rope(_half_rmsnorm(zq_ref[0, :, cs], qg_ref[...]), cos_ref[...], sa_ref[...], sb_ref[...])
            q2_s[h * rows_h:(h + 1) * rows_h, :] = _stack_half_masked(q * (DK_HALF ** -0.5)).astype(BF16)
            k = _att_rope(_half_rmsnorm(zk_ref[0, :, cs], kg_ref[...]), cos_ref[...], sa_ref[...], sb_ref[...])
            newk_ref[0, :, cs] = k
            kn_s[0:n_new, cs] = k
        v = zv_ref[0]
        newv_ref[0] = v
        vn_s[0:n_new, :] = v
        m_s[...] = jnp.full(m_s.shape, NEG_BIG, F32)
        l_s[...] = jnp.zeros(l_s.shape, F32)
        acc_s[...] = jnp.zeros(acc_s.shape, F32)

    def attend(load_k, load_v, mask):
        s = jnp.concatenate(
            [lax.dot_general(q2_s[h * rows_h:(h + 1) * rows_h, :], load_k(h), (((1,), (1,)), ((), ())),
                             preferred_element_type=F32) for h in range(h_att)], axis=0)
        if mask is not None:
            s = jnp.where(mask, s, NEG_BIG)
        m_prev = m_s[...]
        m_new = jnp.maximum(m_prev, jnp.max(s, axis=-1, keepdims=True))
        alpha = jnp.exp(m_prev - m_new)
        pr = jnp.exp(s - m_new)
        l_s[...] = alpha * l_s[...] + jnp.sum(pr, axis=-1, keepdims=True)
        pb = pr.astype(BF16)
        pv = jnp.concatenate(
            [jnp.dot(pb[h * rows_h:(h + 1) * rows_h, :], load_v(h), preferred_element_type=F32)
             for h in range(h_att)], axis=0)
        acc_s[...] = alpha * acc_s[...] + pv
        m_s[...] = m_new

    def head_cols(ref, h, idx=None):
        cs = slice(h * HEAD_DIM, (h + 1) * HEAD_DIM)
        return ref[:, cs] if idx is None else ref[idx, :, cs]

    attend(lambda h: head_cols(kc_ref, h).astype(BF16), lambda h: head_cols(vc_ref, h).astype(BF16), None)

    @pl.when(p == pl.num_programs(1) - 1)
    def _():
        shape = (h_att * rows_h, PAGE_SIZE)
        row = lax.broadcasted_iota(jnp.int32, shape, 0)
        q_tok = lax.rem(row, n_new)
        key = lax.broadcasted_iota(jnp.int32, shape, 1)
        attend(lambda h: head_cols(kn_s, h).astype(BF16), lambda h: head_cols(vn_s, h).astype(BF16),
               key <= q_tok)
        o_full = acc_s[...] / l_s[...]
        for h in range(h_att):
            o1 = o_full[h * rows_h:h * rows_h + n_new]
            o2 = o_full[h * rows_h + n_new:(h + 1) * rows_h]
            o = o1 - lam_ref[...] * o2
            o_ref[0, :, h * HEAD_DIM:(h + 1) * HEAD_DIM] = _sub_layer_norm(o, sg_ref[...], out_scale)


def _sample_attention(z3, cache_k4, cache_v4, layer, page_table, w_att, tabs, qg, kg, sg, lam_row, out_scale):
    bsz, n_new, _ = z3.shape
    n_pages = page_table.shape[1]
    h_att = w_att // HEAD_DIM
    cos, sa, sb = tabs
    z_spec = lambda c: pl.BlockSpec((1, n_new, w_att), functools.partial(lambda b, p, pt, c: (b, 0, c), c=c))
    cache_spec = pl.BlockSpec((None, None, PAGE_SIZE, w_att), lambda b, p, pt: (layer, pt[b, p], 0, 0))
    tab_spec = pl.BlockSpec((n_new, HEAD_DIM), lambda b, p, pt: (0, 0))
    row_spec = pl.BlockSpec((1, HEAD_DIM), lambda b, p, pt: (0, 0))
    out_spec = pl.BlockSpec((1, n_new, w_att), lambda b, p, pt: (b, 0, 0))
    rows = h_att * 2 * n_new
    grid_spec = pltpu.PrefetchScalarGridSpec(
        num_scalar_prefetch=1,
        grid=(bsz, n_pages),
        in_specs=[z_spec(0), z_spec(1), z_spec(2), cache_spec, cache_spec, tab_spec, tab_spec, tab_spec,
                  row_spec, row_spec, row_spec, row_spec],
        out_specs=(out_spec, out_spec, out_spec),
        scratch_shapes=[pltpu.VMEM((rows, HEAD_DIM), BF16),
                        pltpu.VMEM((PAGE_SIZE, w_att), F32), pltpu.VMEM((PAGE_SIZE, w_att), F32),
                        pltpu.VMEM((rows, 1), F32), pltpu.VMEM((rows, 1), F32),
                        pltpu.VMEM((rows, HEAD_DIM), F32)],
    )
    return pl.pallas_call(
        functools.partial(_sample_attn_kernel, h_att=h_att, n_new=n_new, out_scale=out_scale),
        out_shape=(jax.ShapeDtypeStruct((bsz, n_new, w_att), F32),) * 3,
        grid_spec=grid_spec,
        compiler_params=_params(("parallel", "arbitrary")),
        name="sample_paged_diff_attention",
    )(page_table, z3, z3, z3, cache_k4, cache_v4, cos, sa, sb, qg, kg, sg, lam_row)


def _retention_kernel(zq_ref, zk_ref, zv_ref, zg_ref, cos_ref, sin_ref, lg_ref, s0_ref,
                      o_ref, sfin_ref, *pad_s, chunk, n_chunks):
    cp = RET_CHUNK if chunk < RET_CHUNK else chunk
    lg = lg_ref[0]
    ri = lax.broadcasted_iota(jnp.int32, (cp, cp), 0)
    ci = lax.broadcasted_iota(jnp.int32, (cp, cp), 1)
    diff = (ri - ci).astype(F32)
    dec = jnp.where(diff >= 0, jnp.exp(jnp.maximum(diff, 0.0) * lg), 0.0)
    idx = lax.broadcasted_iota(jnp.int32, (cp, HEAD_DIM), 0).astype(F32)
    cross = jnp.exp((idx + 1.0) * lg)
    kdec_w = jnp.exp((chunk - 1.0 - idx) * lg)
    chunk_decay = jnp.exp(chunk * lg)

    def load(ref, r0):
        x = ref[pl.ds(r0, chunk), :]
        if cp == chunk:
            return x
        buf = pad_s[0]
        buf[...] = jnp.zeros(buf.shape, F32)
        buf[0:chunk, :] = x
        return buf[...]

    def body(c, state):
        r0 = pl.multiple_of(c * chunk, chunk)
        cos = load(cos_ref, r0)
        sin = load(sin_ref, r0)
        q = _ret_rope(load(zq_ref, r0), cos, sin)
        k = _ret_rope(load(zk_ref, r0), cos, sin) * (HEAD_DIM ** -0.5)
        v = load(zv_ref, r0)
        qb, kb, vb = q.astype(BF16), k.astype(BF16), v.astype(BF16)
        s = lax.dot_general(qb, kb, (((1,), (1,)), ((), ())), preferred_element_type=F32) * dec
        o = jnp.dot(s.astype(BF16), vb, preferred_element_type=F32)
        o = o + jnp.dot(qb, state.astype(BF16), preferred_element_type=F32) * cross
        kd = (k * kdec_w).astype(BF16)
        new_state = state * chunk_decay + lax.dot_general(kd, vb, (((0,), (0,)), ((), ())),
                                                          preferred_element_type=F32)
        ms = jnp.mean(o * o, axis=-1, keepdims=True)
        g = load(zg_ref, r0)
        out = o * lax.rsqrt(ms + EPS) * (g * _sigmoid(g))
        o_ref[pl.ds(r0, chunk), :] = out[0:chunk].astype(o_ref.dtype)
        return new_state

    sfin_ref[0, 0] = lax.fori_loop(0, n_chunks, body, s0_ref[0, 0])


def _retention(z, b, l, col0, h_ret, tabs, log_gamma_rows, s0):
    m = z.shape[0]
    chunk = RET_CHUNK if l % RET_CHUNK == 0 else l
    n_chunks = l // chunk
    cos, sin = tabs
    zs = lambda sec: pl.BlockSpec((l, HEAD_DIM), functools.partial(
        lambda bb, h, sec: (bb, col0 + sec * h_ret + h), sec=sec))
    tab_spec = pl.BlockSpec((l, HEAD_DIM), lambda bb, h: (0, 0))
    scratch = [] if chunk == RET_CHUNK else [pltpu.VMEM((RET_CHUNK, HEAD_DIM), F32)]
    return pl.pallas_call(
        functools.partial(_retention_kernel, chunk=chunk, n_chunks=n_chunks),
        out_shape=(jax.ShapeDtypeStruct((m, h_ret * HEAD_DIM), F32 if chunk < 16 else BF16),
                   jax.ShapeDtypeStruct((b, h_ret, HEAD_DIM, HEAD_DIM), F32)),
        grid=(b, h_ret),
        in_specs=[zs(0), zs(1), zs(2), zs(3), tab_spec, tab_spec,
                  pl.BlockSpec((1, 1, HEAD_DIM), lambda bb, h: (h, 0, 0)),
                  pl.BlockSpec((1, 1, HEAD_DIM, HEAD_DIM), lambda bb, h: (bb, h, 0, 0))],
        out_specs=(pl.BlockSpec((l, HEAD_DIM), lambda bb, h: (bb, h)),
                   pl.BlockSpec((1, 1, HEAD_DIM, HEAD_DIM), lambda bb, h: (bb, h, 0, 0))),
        scratch_shapes=scratch,
        compiler_params=_params(("parallel", "parallel")),
        name="retention",
    )(z, z, z, z, cos, sin, log_gamma_rows, s0)


def _s5_kernel(u_ref, bb_ref, cc_ref, d_ref, are_ref, aim_ref, h0_ref, y_ref, xfin_ref, bu_s, x_s, *, steps):
    i = pl.program_id(1)
    nc = S5_STATE_COLS

    @pl.when(i == 0)
    def _():
        x_s[...] = h0_ref[0]

    u = u_ref[...]
    bu_s[...] = jnp.dot(u.astype(BF16), bb_ref[0], preferred_element_type=F32)
    a_re = jnp.broadcast_to(are_ref[0], (SUBLANES, nc))
    a_im = jnp.broadcast_to(aim_ref[0], (SUBLANES, nc))

    def body(t, carry):
        xr, xi = carry
        r0 = pl.multiple_of(t * SUBLANES, SUBLANES)
        nr = a_re * xr - a_im * xi + bu_s[pl.ds(r0, SUBLANES), 0:nc]
        ni = a_re * xi + a_im * xr + bu_s[pl.ds(r0, SUBLANES), nc:2 * nc]
        bu_s[pl.ds(r0, SUBLANES), 0:nc] = nr
        bu_s[pl.ds(r0, SUBLANES), nc:2 * nc] = ni
        return nr, ni

    xr, xi = lax.fori_loop(0, steps, body, (x_s[:, 0:nc], x_s[:, nc:2 * nc]), unroll=min(8, steps))
    x_s[:, 0:nc] = xr
    x_s[:, nc:2 * nc] = xi
    y_ref[...] = jnp.dot(bu_s[...].astype(BF16), cc_ref[0], preferred_element_type=F32) + d_ref[...] * u

    @pl.when(i == pl.num_programs(1) - 1)
    def _():
        xfin_ref[0] = x_s[...]


def _s5(u_rows, n_steps, bb, cc, d_row, a_re, a_im, h0):
    rows, w_ssm = u_rows.shape
    n_blocks = w_ssm // LANES
    t_chunk = _pick_tile(n_steps, (256, 128, 64, 32, 16, 8))
    r_chunk = t_chunk * SUBLANES
    return pl.pallas_call(
        functools.partial(_s5_kernel, steps=t_chunk),
        out_shape=(jax.ShapeDtypeStruct((rows, w_ssm), F32),
                   jax.ShapeDtypeStruct((n_blocks, SUBLANES, 2 * S5_STATE_COLS), F32)),
        grid=(n_blocks, n_steps // t_chunk),
        in_specs=[pl.BlockSpec((r_chunk, LANES), lambda j, i: (i, j)),
                  pl.BlockSpec((1, LANES, 2 * S5_STATE_COLS), lambda j, i: (j, 0, 0)),
                  pl.BlockSpec((1, 2 * S5_STATE_COLS, LANES), lambda j, i: (j, 0, 0)),
                  pl.BlockSpec((1, LANES), lambda j, i: (0, j)),
                  pl.BlockSpec((1, 1, S5_STATE_COLS), lambda j, i: (j, 0, 0)),
                  pl.BlockSpec((1, 1, S5_STATE_COLS), lambda j, i: (j, 0, 0)),
                  pl.BlockSpec((1, SUBLANES, 2 * S5_STATE_COLS), lambda j, i: (j, 0, 0))],
        out_specs=(pl.BlockSpec((r_chunk, LANES), lambda j, i: (i, j)),
                   pl.BlockSpec((1, SUBLANES, 2 * S5_STATE_COLS), lambda j, i: (j, 0, 0))),
        scratch_shapes=[pltpu.VMEM((r_chunk, 2 * S5_STATE_COLS), F32),
                        pltpu.VMEM((SUBLANES, 2 * S5_STATE_COLS), F32)],
        compiler_params=_params(("parallel", "arbitrary")),
        name="s5_scan",
    )(u_rows, bb, cc, d_row, a_re, a_im, h0)


def _s5_post_kernel(y_ref, w_ref, g_ref, o_ref):
    y = y_ref[...]
    y = 0.5 * y * (1.0 + jnp.tanh(math.sqrt(2.0 / math.pi) * (y + 0.044715 * (y * y * y))))
    gate = _sigmoid(jnp.dot(y.astype(BF16), w_ref[...], preferred_element_type=F32))
    y = y * gate
    ms = jnp.mean(y * y, axis=-1, keepdims=True)
    o_ref[...] = (y * lax.rsqrt(ms + EPS) * g_ref[...]).astype(o_ref.dtype)


def _s5_post(y, w_glu, g):
    m, w = y.shape
    tm = _pick_tile(m, (512, 256, 128, 64))
    return pl.pallas_call(
        _s5_post_kernel,
        out_shape=jax.ShapeDtypeStruct((m, w), BF16),
        grid=(m // tm,),
        in_specs=[pl.BlockSpec((tm, w), lambda i: (i, 0)), pl.BlockSpec((w, w), lambda i: (0, 0)),
                  pl.BlockSpec((1, w), lambda i: (0, 0))],
        out_specs=pl.BlockSpec((tm, w), lambda i: (i, 0)),
        compiler_params=_params(("parallel",)),
        name="s5_gelu_glu_norm",
    )(y, w_glu, g.reshape(1, w))


def _s5_discretise(lam_re, lam_im, log_dt, b_re, b_im, c_re, c_im):
    g_cnt = lam_re.shape[0]
    dt = jnp.exp(log_dt.astype(F32))[:, None]
    mag = jnp.exp(lam_re * dt)
    ang = lam_im * dt
    a_re, a_im = mag * jnp.cos(ang), mag * jnp.sin(ang)
    den = lam_re * lam_re + lam_im * lam_im
    num_re = a_re - 1.0
    coef_re = (num_re * lam_re + a_im * lam_im) / den
    coef_im = (a_im * lam_re - num_re * lam_im) / den
    bb_re = coef_re[..., None] * b_re - coef_im[..., None] * b_im
    bb_im = coef_re[..., None] * b_im + coef_im[..., None] * b_re
    nb = g_cnt // S5_GROUPS_PER_BLOCK
    gb = S5_GROUPS_PER_BLOCK
    eye = jnp.eye(gb, dtype=F32)

    def block_diag_in(w):
        w = w.reshape(nb, gb, SSM_P, SSM_H)
        return jnp.einsum('ngph,gk->nghkp', w, eye).reshape(nb, gb * SSM_H, gb * SSM_P)

    def block_diag_out(w):
        w = w.reshape(nb, gb, SSM_H, SSM_P)
        return jnp.einsum('nghp,gk->ngpkh', w, eye).reshape(nb, gb * SSM_P, gb * SSM_H)

    bb = jnp.concatenate([block_diag_in(bb_re), block_diag_in(bb_im)], axis=2).astype(BF16)
    cc = jnp.concatenate([block_diag_out(c_re.astype(F32)), block_diag_out(-c_im.astype(F32))], axis=1).astype(BF16)
    a_re_b = a_re.reshape(nb, 1, gb * SSM_P)
    a_im_b = a_im.reshape(nb, 1, gb * SSM_P)
    return bb, cc, a_re_b, a_im_b


def _s5_mixer(z, b, l, col0, w_ssm, prm, h0_re, h0_im):
    nb = w_ssm // LANES
    gb = S5_GROUPS_PER_BLOCK
    u = z[:, col0:col0 + w_ssm].reshape(b, l, w_ssm).transpose(1, 0, 2)
    u = jnp.pad(u, ((0, 0), (0, SUBLANES - b), (0, 0))).reshape(l * SUBLANES, w_ssm)

    def pack_state(h):
        h = jnp.pad(h.astype(F32), ((0, SUBLANES - b), (0, 0), (0, 0)))
        return h.reshape(SUBLANES, nb, gb * SSM_P).transpose(1, 0, 2)

    h0 = jnp.concatenate([pack_state(h0_re), pack_state(h0_im)], axis=2)
    y, xfin = _s5(u, l, prm['bb'], prm['cc'], prm['d'], prm['a_re'], prm['a_im'], h0)
    o = _s5_post(y, prm['w_glu'], prm['norm_g'])
    o = o.reshape(l, SUBLANES, w_ssm)[:, :b].transpose(1, 0, 2).reshape(b * l, w_ssm)

    def unpack_state(x):
        return x.transpose(1, 0, 2).reshape(SUBLANES, nb * gb, SSM_P)[:b]

    return o, unpack_state(xfin[:, :, :S5_STATE_COLS]), unpack_state(xfin[:, :, S5_STATE_COLS:])


def _att_rope_tables(pos):
    freqs = ROPE_THETA ** (-jnp.arange(0, DK_HALF, 2, dtype=F32) / DK_HALF)
    ang = pos.astype(F32)[:, None] * freqs[None, :]
    c, s = jnp.cos(ang), jnp.sin(ang)
    zero = jnp.zeros_like(s)
    return (jnp.concatenate([c, c, c, c], axis=-1),
            jnp.concatenate([-s, zero, -s, zero], axis=-1),
            jnp.concatenate([zero, s, zero, s], axis=-1))


def _ret_rope_tables(pos):
    freqs = 1.0 / (ROPE_THETA ** jnp.linspace(0.0, 1.0, HEAD_DIM // 2, dtype=F32))
    ang = pos.astype(F32)[:, None] * freqs[None, :]
    c, s = jnp.cos(ang), jnp.sin(ang)
    return jnp.concatenate([c, c], axis=-1), jnp.concatenate([-s, s], axis=-1)


def _row128(v):
    return jnp.tile(v.astype(F32), HEAD_DIM // v.shape[0]).reshape(1, HEAD_DIM)


def _layer(x, b, l, lw, attend, ret_tabs, ret_s0, s5_h0_re, s5_h0_im):
    d = x.shape[1]
    w_att = lw['w_att']
    h = _rmsnorm(x, lw['norm_mix_g'])
    z = _matmul([h], lw['w_in'], [0], tm=1024, tn=512, name="in_proj")
    o_att, new_k, new_v = attend(z)
    o_ret, s_ret = _retention(z, b, l, 3 * w_att // HEAD_DIM, w_att // HEAD_DIM, ret_tabs, lw['log_gamma'], ret_s0)
    o_ssm, s5_re, s5_im = _s5_mixer(z, b, l, 7 * w_att, d - 2 * w_att, lw['s5'], s5_h0_re, s5_h0_im)
    ssm_rb = (2 * w_att) // (d - 2 * w_att)
    x = _matmul([o_att, o_ret.astype(BF16), o_ssm], lw['w_out'], [0, 1, ssm_rb], mode="res", res=x,
                tm=1024, tn=512, name="out_proj")
    h2 = _rmsnorm(x, lw['norm_ffn_g'])
    act = _matmul([h2], lw['w_gate'], [0], mode="swiglu", b2=lw['w_up'], out_dtype=BF16,
                  tm=1024, tn=256, name="ffn_gate_up")
    x = _matmul([act], lw['w_down'], [0], mode="res", res=x, tm=512, tn=256, name="ffn_down")
    return x, new_k, new_v, s_ret, s5_re, s5_im


def kernel(x_prompt, x_sample, cache_k, cache_v, page_table, state_ret, state_s5_re, state_s5_im,
           norm_mix_g, w_in, q_norm_g, k_norm_g, lambda_q1, lambda_k1, lambda_q2, lambda_k2, subln_g,
           s5_lambda_re, s5_lambda_im, s5_log_dt, s5_b_re, s5_b_im, s5_c_re, s5_c_im, s5_d, s5_w_glu,
           s5_norm_g, w_out, norm_ffn_g, w_gate, w_up, w_down):
    b_p, seq, d = x_prompt.shape
    b_s, n_new, _ = x_sample.shape
    depth = w_in.shape[0]
    w_att = (3 * d) // (8 * HEAD_DIM) * HEAD_DIM
    h_att = w_att // HEAD_DIM
    w_ssm = d - 2 * w_att
    n_pages = page_table.shape[1]
    past = n_pages * PAGE_SIZE
    assert (2 * w_att) % w_ssm == 0 and w_ssm % LANES == 0 and b_p <= SUBLANES and b_s <= SUBLANES

    pos_p = jnp.arange(seq, dtype=jnp.int32)
    pos_s = past + jnp.arange(n_new, dtype=jnp.int32)
    att_tabs_p, att_tabs_s = _att_rope_tables(pos_p), _att_rope_tables(pos_s)
    ret_tabs_p, ret_tabs_s = _ret_rope_tables(pos_p), _ret_rope_tables(pos_s)
    log_gamma = jnp.log(1.0 - 2.0 ** (-5.0 - jnp.arange(h_att, dtype=F32)))
    log_gamma_rows = jnp.broadcast_to(log_gamma[:, None, None], (h_att, 1, HEAD_DIM))
    cache_k4 = cache_k.reshape(cache_k.shape[0], cache_k.shape[1], PAGE_SIZE, w_att)
    cache_v4 = cache_v.reshape(cache_v.shape[0], cache_v.shape[1], PAGE_SIZE, w_att)

    xp = x_prompt.reshape(b_p * seq, d)
    xs = x_sample.reshape(b_s * n_new, d)
    zeros_ret = jnp.zeros((b_p, h_att, HEAD_DIM, HEAD_DIM), F32)
    zeros_s5 = jnp.zeros((b_p, w_ssm // SSM_H, SSM_P), F32)
    outs = [[] for _ in range(10)]
    for layer in range(depth):
        lam_init = 0.8 - 0.6 * math.exp(-0.3 * layer)
        lam = (jnp.exp(jnp.sum(lambda_q1[layer].astype(F32) * lambda_k1[layer].astype(F32)))
               - jnp.exp(jnp.sum(lambda_q2[layer].astype(F32) * lambda_k2[layer].astype(F32))) + lam_init)
        lam_row = jnp.broadcast_to(lam.reshape(1, 1), (1, HEAD_DIM)).astype(F32)
        bb, cc, a_re, a_im = _s5_discretise(
            s5_lambda_re[layer].astype(F32), s5_lambda_im[layer].astype(F32), s5_log_dt[layer],
            s5_b_re[layer].astype(F32), s5_b_im[layer].astype(F32), s5_c_re[layer], s5_c_im[layer])
        lw = {
            'w_att': w_att,
            'norm_mix_g': norm_mix_g[layer], 'norm_ffn_g': norm_ffn_g[layer],
            'w_in': w_in[layer].astype(BF16), 'w_out': w_out[layer].astype(BF16),
            'w_gate': w_gate[layer].astype(BF16), 'w_up': w_up[layer].astype(BF16),
            'w_down': w_down[layer].astype(BF16),
            'log_gamma': log_gamma_rows,
            's5': {'bb': bb, 'cc': cc, 'a_re': a_re, 'a_im': a_im,
                   'd': s5_d[layer].astype(F32).reshape(1, w_ssm),
                   'w_glu': s5_w_glu[layer].astype(BF16), 'norm_g': s5_norm_g[layer].astype(F32)},
        }
        qg, kg, sg = _row128(q_norm_g[layer]), _row128(k_norm_g[layer]), _row128(subln_g[layer])
        out_scale = 1.0 - lam_init

        attend_p = functools.partial(_prompt_attention, b=b_p, l=seq, w_att=w_att, tabs=att_tabs_p,
                                     qg=qg, kg=kg, sg=sg, lam_row=lam_row, out_scale=out_scale)
        xp, kp, vp, rp, p_re, p_im = _layer(xp, b_p, seq, lw, attend_p, ret_tabs_p, zeros_ret, zeros_s5, zeros_s5)

        def attend_s(z):
            o, nk, nv = _sample_attention(z.reshape(b_s, n_new, z.shape[1]), cache_k4, cache_v4, layer,
                                          page_table, w_att, att_tabs_s, qg, kg, sg, lam_row, out_scale)
            flat = lambda t: t.reshape(b_s * n_new, w_att)
            return flat(o).astype(BF16), flat(nk), flat(nv)

        xs, k_s, v_s, r_s, s_re, s_im = _layer(xs, b_s, n_new, lw, attend_s, ret_tabs_s, state_ret[layer],
                                               state_s5_re[layer], state_s5_im[layer])
        for lst, val in zip(outs, (kp.reshape(b_p, seq, h_att, HEAD_DIM), vp.reshape(b_p, seq, h_att, HEAD_DIM),
                                   k_s.reshape(b_s, n_new, h_att, HEAD_DIM), v_s.reshape(b_s, n_new, h_att, HEAD_DIM),
                                   rp, r_s, p_re, p_im, s_re, s_im)):
            lst.append(val)
    return (xp.reshape(b_p, seq, d), xs.reshape(b_s, n_new, d)) + tuple(jnp.stack(o) for o in outs)
```

```python
import functools
import math

import jax
import jax.numpy as jnp
from jax import lax
from jax.experimental import pallas as pl
from jax.experimental.pallas import tpu as pltpu

F32 = jnp.float32
BF16 = jnp.bfloat16

HEAD_DIM = 128
DK_HALF = HEAD_DIM // 2
SSM_H = 16
SSM_P = 64
PAGE_SIZE = 128
RET_CHUNK = 128
ROPE_THETA = 10000.0
EPS = 1e-6
LANES = 128
SUBLANES = 8
S5_GROUPS_PER_BLOCK = LANES // SSM_H
S5_STATE_COLS = S5_GROUPS_PER_BLOCK * SSM_P
NEG_BIG = -1e30
V7X_VMEM_LIMIT_BYTES = 56 * 1024 * 1024


def _pick_tile(n, candidates):
    for c in candidates:
        if n % c == 0:
            return c
    return n


def _params(semantics):
    return pltpu.CompilerParams(dimension_semantics=semantics, vmem_limit_bytes=V7X_VMEM_LIMIT_BYTES)


def _sigmoid(x):
    return 1.0 / (1.0 + jnp.exp(-x))


def _rmsnorm_kernel(x_ref, g_ref, o_ref):
    x = x_ref[...]
    ms = jnp.mean(x * x, axis=-1, keepdims=True)
    o_ref[...] = (x * lax.rsqrt(ms + EPS) * g_ref[...]).astype(o_ref.dtype)


def _rmsnorm(x, g):
    m, d = x.shape
    tm = _pick_tile(m, (512, 256, 128, 64))
    return pl.pallas_call(
        _rmsnorm_kernel,
        out_shape=jax.ShapeDtypeStruct((m, d), BF16),
        grid=(m // tm,),
        in_specs=[pl.BlockSpec((tm, d), lambda i: (i, 0)), pl.BlockSpec((1, d), lambda i: (0, 0))],
        out_specs=pl.BlockSpec((tm, d), lambda i: (i, 0)),
        compiler_params=_params(("parallel",)),
        name="rmsnorm",
    )(x, g.reshape(1, d))


def _mm_kernel(*refs, n_a, mode):
    a_refs = refs[:n_a]
    o_ref = refs[-1]
    if mode == "swiglu":
        bg_ref, bu_ref = refs[n_a], refs[n_a + 1]
        a = a_refs[0][...]
        g = jnp.dot(a, bg_ref[...], preferred_element_type=F32)
        u = jnp.dot(a, bu_ref[...], preferred_element_type=F32)
        o_ref[...] = (g * _sigmoid(g) * u).astype(o_ref.dtype)
        return
    b_refs = refs[n_a:2 * n_a]
    acc = jnp.dot(a_refs[0][...], b_refs[0][...], preferred_element_type=F32)
    for a_ref, b_ref in zip(a_refs[1:], b_refs[1:]):
        acc = acc + jnp.dot(a_ref[...], b_ref[...], preferred_element_type=F32)
    if mode == "res":
        acc = refs[2 * n_a][...] + acc
    o_ref[...] = acc.astype(o_ref.dtype)


def _matmul(a_list, b, row_blocks, *, mode="plain", res=None, b2=None, out_dtype=F32, tm, tn, name):
    m = a_list[0].shape[0]
    n = b.shape[1]
    tm = _pick_tile(m, (tm, 512, 256, 128, 64))
    tn = _pick_tile(n, (tn, 512, 256, 128))
    in_specs = [pl.BlockSpec((tm, a.shape[1]), lambda i, j: (i, 0)) for a in a_list]
    args = list(a_list)
    for a, rb in zip(a_list, row_blocks):
        in_specs.append(pl.BlockSpec((a.shape[1], tn), functools.partial(lambda i, j, rb: (rb, j), rb=rb)))
        args.append(b)
    if mode == "swiglu":
        in_specs.append(pl.BlockSpec((a_list[0].shape[1], tn), lambda i, j: (0, j)))
        args.append(b2)
    if mode == "res":
        in_specs.append(pl.BlockSpec((tm, tn), lambda i, j: (i, j)))
        args.append(res)
    return pl.pallas_call(
        functools.partial(_mm_kernel, n_a=len(a_list), mode=mode),
        out_shape=jax.ShapeDtypeStruct((m, n), out_dtype),
        grid=(m // tm, n // tn),
        in_specs=in_specs,
        out_specs=pl.BlockSpec((tm, tn), lambda i, j: (i, j)),
        compiler_params=_params(("parallel", "arbitrary")),
        name=name,
    )(*args)


def _mm_pair_kernel(*refs, n_a, n_w, mode):
    ap_refs, as_refs = refs[:n_a], refs[n_a:2 * n_a]
    w_refs = refs[2 * n_a:2 * n_a + n_w]
    pos = 2 * n_a + n_w
    res_refs = refs[pos:pos + 2] if mode == "res" else ()
    pos += len(res_refs)
    op_ref, os_ref = refs[pos], refs[pos + 1]
    wb_refs = refs[pos + 2:]
    first = pl.program_id(1) == 0

    @pl.when(first)
    def _():
        for w_ref, wb_ref in zip(w_refs, wb_refs):
            wb_ref[...] = w_ref[...].astype(BF16)

    def product(a_refs, res_ref):
        if mode == "swiglu":
            a = a_refs[0][...]
            g = jnp.dot(a, wb_refs[0][...], preferred_element_type=F32)
            u = jnp.dot(a, wb_refs[1][...], preferred_element_type=F32)
            return g * _sigmoid(g) * u
        acc = jnp.dot(a_refs[0][...], wb_refs[0][...], preferred_element_type=F32)
        for a_ref, wb_ref in zip(a_refs[1:], wb_refs[1:]):
            acc = acc + jnp.dot(a_ref[...], wb_ref[...], preferred_element_type=F32)
        return acc if res_ref is None else res_ref[...] + acc

    op_ref[...] = product(ap_refs, res_refs[0] if res_refs else None).astype(op_ref.dtype)

    @pl.when(first)
    def _():
        os_ref[...] = product(as_refs, res_refs[1] if res_refs else None).astype(os_ref.dtype)


def _matmul_pair(ap_list, as_list, w, row_blocks, *, mode="plain", res_p=None, res_s=None, w2=None,
                 out_dtype=F32, tm, tn, name):
    mp, ms = ap_list[0].shape[0], as_list[0].shape[0]
    n = w.shape[1]
    tm = _pick_tile(mp, (tm, 512, 256, 128, 64))
    tn = _pick_tile(n, (tn, 512, 256, 128))
    in_specs = [pl.BlockSpec((tm, a.shape[1]), lambda j, i: (i, 0)) for a in ap_list]
    in_specs += [pl.BlockSpec((ms, a.shape[1]), lambda j, i: (0, 0)) for a in as_list]
    args = list(ap_list) + list(as_list)
    scratch = []
    weights = [(w, a.shape[1], rb) for a, rb in zip(ap_list, row_blocks)]
    if mode == "swiglu":
        weights.append((w2, ap_list[0].shape[1], 0))
    for wt, rows, rb in weights:
        in_specs.append(pl.BlockSpec((rows, tn), functools.partial(lambda j, i, rb: (rb, j), rb=rb)))
        args.append(wt)
        scratch.append(pltpu.VMEM((rows, tn), BF16))
    if mode == "res":
        in_specs += [pl.BlockSpec((tm, tn), lambda j, i: (i, j)), pl.BlockSpec((ms, tn), lambda j, i: (0, j))]
        args += [res_p, res_s]
    return pl.pallas_call(
        functools.partial(_mm_pair_kernel, n_a=len(ap_list), n_w=len(weights), mode=mode),
        out_shape=(jax.ShapeDtypeStruct((mp, n), out_dtype), jax.ShapeDtypeStruct((ms, n), out_dtype)),
        grid=(n // tn, mp // tm),
        in_specs=in_specs,
        out_specs=(pl.BlockSpec((tm, tn), lambda j, i: (i, j)), pl.BlockSpec((ms, tn), lambda j, i: (0, j))),
        scratch_shapes=scratch,
        compiler_params=_params(("parallel", "arbitrary")),
        name=name,
    )(*args)


def _half_rmsnorm(x, g):
    lane = lax.broadcasted_iota(jnp.int32, x.shape, 1)
    first = lane < DK_HALF
    sq = x * x
    tot = jnp.sum(sq, axis=-1, keepdims=True)
    s1 = jnp.sum(jnp.where(first, sq, 0.0), axis=-1, keepdims=True)
    ms = jnp.where(first, s1, tot - s1) * (1.0 / DK_HALF)
    return x * lax.rsqrt(ms + EPS) * g


def _att_rope(x, cos, sin_a, sin_b):
    return x * cos + pltpu.roll(x, LANES - DK_HALF // 2, 1) * sin_a + pltpu.roll(x, DK_HALF // 2, 1) * sin_b


def _ret_rope(x, cos, sin_signed):
    return x * cos + pltpu.roll(x, DK_HALF, 1) * sin_signed


def _stack_half_masked(q):
    lane = lax.broadcasted_iota(jnp.int32, q.shape, 1)
    first = lane < DK_HALF
    return jnp.concatenate([jnp.where(first, q, 0.0), jnp.where(first, 0.0, q)], axis=0)


def _sub_layer_norm(o, g, out_scale):
    ms = jnp.mean(o * o, axis=-1, keepdims=True)
    return o * lax.rsqrt(ms + EPS) * g * out_scale


def _prompt_attn_kernel(zq_ref, zk_ref, zv_ref, cq_ref, saq_ref, sbq_ref, ck_ref, sak_ref, sbk_ref,
                        qg_ref, kg_ref, sg_ref, lam_ref,
                        o_ref, newk_ref, newv_ref,
                        k_s, vt_s, q2_s, m_s, l_s, acc_s, *, tq, n_tiles, out_scale):
    i = pl.program_id(2)

    @pl.when(i == 0)
    def _():
        k = _att_rope(_half_rmsnorm(zk_ref[...], kg_ref[...]), ck_ref[...], sak_ref[...], sbk_ref[...])
        newk_ref[...] = k
        k_s[...] = k.astype(BF16)
        v = zv_ref[...]
        newv_ref[...] = v
        for t in range(n_tiles):
            vt_s[t] = v[t * tq:(t + 1) * tq].T.astype(BF16)

    q = _att_rope(_half_rmsnorm(zq_ref[...], qg_ref[...]), cq_ref[...], saq_ref[...], sbq_ref[...])
    q2_s[...] = _stack_half_masked(q * (DK_HALF ** -0.5)).astype(BF16)
    m_s[...] = jnp.full(m_s.shape, NEG_BIG, F32)
    l_s[...] = jnp.zeros(l_s.shape, F32)
    acc_s[...] = jnp.zeros(acc_s.shape, F32)

    def scores(j):
        kj = k_s[pl.ds(pl.multiple_of(j * tq, tq), tq), :]
        return lax.dot_general(kj, q2_s[...], (((1,), (1,)), ((), ())), preferred_element_type=F32)

    def update(s, j):
        m_prev = m_s[...]
        m_new = jnp.maximum(m_prev, jnp.max(s, axis=0, keepdims=True))
        alpha = jnp.exp(m_prev - m_new)
        p = jnp.exp(s - m_new)
        l_s[...] = alpha * l_s[...] + jnp.sum(p, axis=0, keepdims=True)
        acc_s[...] = alpha * acc_s[...] + jnp.dot(vt_s[j], p.astype(BF16), preferred_element_type=F32)
        m_s[...] = m_new

    def body(j, carry):
        update(scores(j), j)
        return carry

    lax.fori_loop(0, i, body, 0)
    s = scores(i)
    key = lax.broadcasted_iota(jnp.int32, s.shape, 0)
    qry = lax.broadcasted_iota(jnp.int32, s.shape, 1)
    qry = jnp.where(qry >= tq, qry - tq, qry)
    update(jnp.where(key <= qry, s, NEG_BIG), i)

    o_full = acc_s[...] / l_s[...]
    o_t = o_full[:, :tq] - lam_ref[0:1, 0:1] * o_full[:, tq:]
    o_ref[...] = _sub_layer_norm(o_t.T, sg_ref[...], out_scale).astype(o_ref.dtype)


def _prompt_attention(z, b, l, w_att, tabs, qg, kg, sg, lam_row, out_scale):
    m = z.shape[0]
    h_att = w_att // HEAD_DIM
    tq = _pick_tile(l, (512, 256, 128))
    nq = l // tq
    cos, sa, sb = tabs
    q_spec = pl.BlockSpec((tq, HEAD_DIM), lambda bb, h, i: (bb * nq + i, h))
    k_spec = pl.BlockSpec((l, HEAD_DIM), lambda bb, h, i: (bb, h_att + h))
    v_spec = pl.BlockSpec((l, HEAD_DIM), lambda bb, h, i: (bb, 2 * h_att + h))
    tq_spec = pl.BlockSpec((tq, HEAD_DIM), lambda bb, h, i: (i, 0))
    tk_spec = pl.BlockSpec((l, HEAD_DIM), lambda bb, h, i: (0, 0))
    row_spec = pl.BlockSpec((1, HEAD_DIM), lambda bb, h, i: (0, 0))
    kv_out_spec = pl.BlockSpec((l, HEAD_DIM), lambda bb, h, i: (bb, h))
    return pl.pallas_call(
        functools.partial(_prompt_attn_kernel, tq=tq, n_tiles=nq, out_scale=out_scale),
        out_shape=(jax.ShapeDtypeStruct((m, w_att), BF16),
                   jax.ShapeDtypeStruct((m, w_att), F32),
                   jax.ShapeDtypeStruct((m, w_att), F32)),
        grid=(b, h_att, nq),
        in_specs=[q_spec, k_spec, v_spec, tq_spec, tq_spec, tq_spec, tk_spec, tk_spec, tk_spec,
                  row_spec, row_spec, row_spec, row_spec],
        out_specs=(q_spec, kv_out_spec, kv_out_spec),
        scratch_shapes=[pltpu.VMEM((l, HEAD_DIM), BF16), pltpu.VMEM((nq, HEAD_DIM, tq), BF16),
                        pltpu.VMEM((2 * tq, HEAD_DIM), BF16),
                        pltpu.VMEM((1, 2 * tq), F32), pltpu.VMEM((1, 2 * tq), F32),
                        pltpu.VMEM((HEAD_DIM, 2 * tq), F32)],
        compiler_params=_params(("parallel", "parallel", "arbitrary")),
        name="prompt_diff_attention",
    )(z, z, z, cos, sa, sb, cos, sa, sb, qg, kg, sg, lam_row)


def _sample_attn_kernel(*refs, h_att, n_new, pages_per_step, out_scale):
    g_cnt = pages_per_step
    zq_ref, zk_ref, zv_ref = refs[1:4]
    kc_refs = refs[4:4 + g_cnt]
    vc_refs = refs[4 + g_cnt:4 + 2 * g_cnt]
    cos_ref, sa_ref, sb_ref, qg_ref, kg_ref, sg_ref, lam_ref = refs[4 + 2 * g_cnt:11 + 2 * g_cnt]
    o_ref, newk_ref, newv_ref, q2_s, kn_s, vn_s, m_s, l_s, acc_s = refs[11 + 2 * g_cnt:]
    p = pl.program_id(1)
    rows_h = 2 * n_new

    @pl.when(p == 0)
    def _():
        kn_s[...] = jnp.zeros(kn_s.shape, F32)
        vn_s[...] = jnp.zeros(vn_s.shape, F32)
        for h in range(h_att):
            cs = slice(h * HEAD_DIM, (h + 1) * HEAD_DIM)
            q = _att_rope(_half_rmsnorm(zq_ref[0, :, cs], qg_ref[...]), cos_ref[...], sa_ref[...], sb_ref[...])
            q2_s[h * rows_h:(h + 1) * rows_h, :] = _stack_half_masked(q * (DK_HALF ** -0.5)).astype(BF16)
            k = _att_rope(_half_rmsnorm(zk_ref[0, :, cs], kg_ref[...]), cos_ref[...], sa_ref[...], sb_ref[...])
            newk_ref[0, :, cs] = k
            kn_s[0:n_new, cs] = k
        v = zv_ref[0]
        newv_ref[0] = v
        vn_s[0:n_new, :] = v
        m_s[...] = jnp.full(m_s.shape, NEG_BIG, F32)
        l_s[...] = jnp.zeros(l_s.shape, F32)
        acc_s[...] = jnp.zeros(acc_s.shape, F32)

    def attend(k_pages, v_pages, mask):
        s = jnp.concatenate(
            [jnp.concatenate(
                [lax.dot_general(q2_s[h * rows_h:(h + 1) * rows_h, :], kp[h], (((1,), (1,)), ((), ())),
                                 preferred_element_type=F32) for kp in k_pages], axis=1)
             for h in range(h_att)], axis=0)
        if mask is not None:
            s = jnp.where(mask, s, NEG_BIG)
        m_prev = m_s[...]
        m_new = jnp.maximum(m_prev, jnp.max(s, axis=-1, keepdims=True))
        alpha = jnp.exp(m_prev - m_new)
        pr = jnp.exp(s - m_new)
        l_s[...] = alpha * l_s[...] + jnp.sum(pr, axis=-1, keepdims=True)
        pb = pr.astype(BF16)
        pv = []
        for h in range(h_att):
            acc = None
            for g, vp in enumerate(v_pages):
                part = jnp.dot(pb[h * rows_h:(h + 1) * rows_h, g * PAGE_SIZE:(g + 1) * PAGE_SIZE], vp[h],
                               preferred_element_type=F32)
                acc = part if acc is None else acc + part
            pv.append(acc)
        acc_s[...] = alpha * acc_s[...] + jnp.concatenate(pv, axis=0)
        m_s[...] = m_new

    head_major = lambda r: jnp.swapaxes(r[...], 0, 1).astype(BF16)
    attend([head_major(r) for r in kc_refs], [head_major(r) for r in vc_refs], None)

    @pl.when(p == pl.num_programs(1) - 1)
    def _():
        shape = (h_att * rows_h, PAGE_SIZE)
        q_tok = lax.rem(lax.broadcasted_iota(jnp.int32, shape, 0), n_new)
        key = lax.broadcasted_iota(jnp.int32, shape, 1)
        head_cols = lambda ref: [ref[:, h * HEAD_DIM:(h + 1) * HEAD_DIM].astype(BF16) for h in range(h_att)]
        attend([head_cols(kn_s)], [head_cols(vn_s)], key <= q_tok)
        o_full = acc_s[...] / l_s[...]
        for h in range(h_att):
            o1 = o_full[h * rows_h:h * rows_h + n_new]
            o2 = o_full[h * rows_h + n_new:(h + 1) * rows_h]
            o = o1 - lam_ref[...] * o2
            o_ref[0, :, h * HEAD_DIM:(h + 1) * HEAD_DIM] = _sub_layer_norm(o, sg_ref[...], out_scale)


def _sample_attention(z3, cache_k, cache_v, layer, page_table, w_att, tabs, qg, kg, sg, lam_row, out_scale):
    bsz, n_new, _ = z3.shape
    n_pages = page_table.shape[1]
    h_att = w_att // HEAD_DIM
    g_cnt = _pick_tile(n_pages, (4, 2, 1))
    cos, sa, sb = tabs
    z_spec = lambda c: pl.BlockSpec((1, n_new, w_att), functools.partial(lambda b, p, pt, c: (b, 0, c), c=c))
    cache_specs = [pl.BlockSpec((None, None, PAGE_SIZE, h_att, HEAD_DIM),
                                functools.partial(lambda b, p, pt, g: (layer, pt[b, p * g_cnt + g], 0, 0, 0), g=g))
                   for g in range(g_cnt)]
    tab_spec = pl.BlockSpec((n_new, HEAD_DIM), lambda b, p, pt: (0, 0))
    row_spec = pl.BlockSpec((1, HEAD_DIM), lambda b, p, pt: (0, 0))
    out_spec = pl.BlockSpec((1, n_new, w_att), lambda b, p, pt: (b, 0, 0))
    rows = h_att * 2 * n_new
    grid_spec = pltpu.PrefetchScalarGridSpec(
        num_scalar_prefetch=1,
        grid=(bsz, n_pages // g_cnt),
        in_specs=[z_spec(0), z_spec(1), z_spec(2)] + cache_specs + cache_specs
                 + [tab_spec, tab_spec, tab_spec, row_spec, row_spec, row_spec, row_spec],
        out_specs=(out_spec, out_spec, out_spec),
        scratch_shapes=[pltpu.VMEM((rows, HEAD_DIM), BF16),
                        pltpu.VMEM((PAGE_SIZE, w_att), F32), pltpu.VMEM((PAGE_SIZE, w_att), F32),
                        pltpu.VMEM((rows, 1), F32), pltpu.VMEM((rows, 1), F32),
                        pltpu.VMEM((rows, HEAD_DIM), F32)],
    )
    return pl.pallas_call(
        functools.partial(_sample_attn_kernel, h_att=h_att, n_new=n_new, pages_per_step=g_cnt,
                          out_scale=out_scale),
        out_shape=(jax.ShapeDtypeStruct((bsz, n_new, w_att), F32),) * 3,
        grid_spec=grid_spec,
        compiler_params=_params(("parallel", "arbitrary")),
        name="sample_paged_diff_attention",
    )(page_table, z3, z3, z3, *([cache_k] * g_cnt), *([cache_v] * g_cnt), cos, sa, sb, qg, kg, sg, lam_row)


def _retention_kernel(zq_ref, zk_ref, zv_ref, zg_ref, cos_ref, sin_ref, lg_ref, s0_ref,
                      o_ref, sfin_ref, *pad_s, chunk, n_chunks):
    cp = RET_CHUNK if chunk < RET_CHUNK else chunk
    lg = lg_ref[0]
    ri = lax.broadcasted_iota(jnp.int32, (cp, cp), 0)
    ci = lax.broadcasted_iota(jnp.int32, (cp, cp), 1)
    diff = (ri - ci).astype(F32)
    dec = jnp.where(diff >= 0, jnp.exp(jnp.maximum(diff, 0.0) * lg), 0.0)
    idx = lax.broadcasted_iota(jnp.int32, (cp, HEAD_DIM), 0).astype(F32)
    cross = jnp.exp((idx + 1.0) * lg)
    kdec_w = jnp.exp((chunk - 1.0 - idx) * lg)
    chunk_decay = jnp.exp(chunk * lg)

    def load(ref, r0):
        x = ref[pl.ds(r0, chunk), :]
        if cp == chunk:
            return x
        buf = pad_s[0]
        buf[...] = jnp.zeros(buf.shape, F32)
        buf[0:chunk, :] = x
        return buf[...]

    def body(c, state):
        r0 = pl.multiple_of(c * chunk, chunk)
        cos = load(cos_ref, r0)
        sin = load(sin_ref, r0)
        q = _ret_rope(load(zq_ref, r0), cos, sin)
        k = _ret_rope(load(zk_ref, r0), cos, sin) * (HEAD_DIM ** -0.5)
        v = load(zv_ref, r0)
        qb, kb, vb = q.astype(BF16), k.astype(BF16), v.astype(BF16)
        s = lax.dot_general(qb, kb, (((1,), (1,)), ((), ())), preferred_element_type=F32) * dec
        o = jnp.dot(s.astype(BF16), vb, preferred_element_type=F32)
        o = o + jnp.dot(qb, state.astype(BF16), preferred_element_type=F32) * cross
        kd = (k * kdec_w).astype(BF16)
        new_state = state * chunk_decay + lax.dot_general(kd, vb, (((0,), (0,)), ((), ())),
                                                          preferred_element_type=F32)
        ms = jnp.mean(o * o, axis=-1, keepdims=True)
        g = load(zg_ref, r0)
        out = o * lax.rsqrt(ms + EPS) * (g * _sigmoid(g))
        o_ref[pl.ds(r0, chunk), :] = out[0:chunk].astype(o_ref.dtype)
        return new_state

    sfin_ref[0, 0] = lax.fori_loop(0, n_chunks, body, s0_ref[0, 0])


def _retention(z, b, l, col0, h_ret, tabs, log_gamma_rows, s0):
    m = z.shape[0]
    chunk = RET_CHUNK if l % RET_CHUNK == 0 else l
    n_chunks = l // chunk
    cos, sin = tabs
    zs = lambda sec: pl.BlockSpec((l, HEAD_DIM), functools.partial(
        lambda bb, h, sec: (bb, col0 + sec * h_ret + h), sec=sec))
    tab_spec = pl.BlockSpec((l, HEAD_DIM), lambda bb, h: (0, 0))
    scratch = [] if chunk == RET_CHUNK else [pltpu.VMEM((RET_CHUNK, HEAD_DIM), F32)]
    return pl.pallas_call(
        functools.partial(_retention_kernel, chunk=chunk, n_chunks=n_chunks),
        out_shape=(jax.ShapeDtypeStruct((m, h_ret * HEAD_DIM), F32 if chunk < 16 else BF16),
                   jax.ShapeDtypeStruct((b, h_ret, HEAD_DIM, HEAD_DIM), F32)),
        grid=(b, h_ret),
        in_specs=[zs(0), zs(1), zs(2), zs(3), tab_spec, tab_spec,
                  pl.BlockSpec((1, 1, HEAD_DIM), lambda bb, h: (h, 0, 0)),
                  pl.BlockSpec((1, 1, HEAD_DIM, HEAD_DIM), lambda bb, h: (bb, h, 0, 0))],
        out_specs=(pl.BlockSpec((l, HEAD_DIM), lambda bb, h: (bb, h)),
                   pl.BlockSpec((1, 1, HEAD_DIM, HEAD_DIM), lambda bb, h: (bb, h, 0, 0))),
        scratch_shapes=scratch,
        compiler_params=_params(("parallel", "parallel")),
        name="retention",
    )(z, z, z, z, cos, sin, log_gamma_rows, s0)


def _s5_kernel(u_ref, bb_ref, cc_ref, d_ref, are_ref, aim_ref, h0_ref, y_ref, xfin_ref, bu_s, x_s, *, steps):
    i = pl.program_id(1)
    nc = S5_STATE_COLS

    @pl.when(i == 0)
    def _():
        x_s[...] = h0_ref[0]

    u = u_ref[...]
    bu_s[...] = jnp.dot(u.astype(BF16), bb_ref[0], preferred_element_type=F32)
    a_re = jnp.broadcast_to(are_ref[0], (SUBLANES, nc))
    a_im = jnp.broadcast_to(aim_ref[0], (SUBLANES, nc))

    def body(t, carry):
        xr, xi = carry
        r0 = pl.multiple_of(t * SUBLANES, SUBLANES)
        nr = a_re * xr - a_im * xi + bu_s[pl.ds(r0, SUBLANES), 0:nc]
        ni = a_re * xi + a_im * xr + bu_s[pl.ds(r0, SUBLANES), nc:2 * nc]
        bu_s[pl.ds(r0, SUBLANES), 0:nc] = nr
        bu_s[pl.ds(r0, SUBLANES), nc:2 * nc] = ni
        return nr, ni

    xr, xi = lax.fori_loop(0, steps, body, (x_s[:, 0:nc], x_s[:, nc:2 * nc]), unroll=min(8, steps))
    x_s[:, 0:nc] = xr
    x_s[:, nc:2 * nc] = xi
    y_ref[...] = jnp.dot(bu_s[...].astype(BF16), cc_ref[0], preferred_element_type=F32) + d_ref[...] * u

    @pl.when(i == pl.num_programs(1) - 1)
    def _():
        xfin_ref[0] = x_s[...]


def _s5(u_rows, n_steps, bb, cc, d_row, a_re, a_im, h0):
    rows, w_ssm = u_rows.shape
    n_blocks = w_ssm // LANES
    t_chunk = _pick_tile(n_steps, (256, 128, 64, 32, 16, 8))
    r_chunk = t_chunk * SUBLANES
    return pl.pallas_call(
        functools.partial(_s5_kernel, steps=t_chunk),
        out_shape=(jax.ShapeDtypeStruct((rows, w_ssm), F32),
                   jax.ShapeDtypeStruct((n_blocks, SUBLANES, 2 * S5_STATE_COLS), F32)),
        grid=(n_blocks, n_steps // t_chunk),
        in_specs=[pl.BlockSpec((r_chunk, LANES), lambda j, i: (i, j)),
                  pl.BlockSpec((1, LANES, 2 * S5_STATE_COLS), lambda j, i: (j, 0, 0)),
                  pl.BlockSpec((1, 2 * S5_STATE_COLS, LANES), lambda j, i: (j, 0, 0)),
                  pl.BlockSpec((1, LANES), lambda j, i: (0, j)),
                  pl.BlockSpec((1, 1, S5_STATE_COLS), lambda j, i: (j, 0, 0)),
                  pl.BlockSpec((1, 1, S5_STATE_COLS), lambda j, i: (j, 0, 0)),
                  pl.BlockSpec((1, SUBLANES, 2 * S5_STATE_COLS), lambda j, i: (j, 0, 0))],
        out_specs=(pl.BlockSpec((r_chunk, LANES), lambda j, i: (i, j)),
                   pl.BlockSpec((1, SUBLANES, 2 * S5_STATE_COLS), lambda j, i: (j, 0, 0))),
        scratch_shapes=[pltpu.VMEM((r_chunk, 2 * S5_STATE_COLS), F32),
                        pltpu.VMEM((SUBLANES, 2 * S5_STATE_COLS), F32)],
        compiler_params=_params(("parallel", "arbitrary")),
        name="s5_scan",
    )(u_rows, bb, cc, d_row, a_re, a_im, h0)


def _s5_post_kernel(y_ref, w_ref, g_ref, o_ref):
    y = y_ref[...]
    y = 0.5 * y * (1.0 + jnp.tanh(math.sqrt(2.0 / math.pi) * (y + 0.044715 * (y * y * y))))
    gate = _sigmoid(jnp.dot(y.astype(BF16), w_ref[...], preferred_element_type=F32))
    y = y * gate
    ms = jnp.mean(y * y, axis=-1, keepdims=True)
    o_ref[...] = (y * lax.rsqrt(ms + EPS) * g_ref[...]).astype(o_ref.dtype)


def _s5_post(y, w_glu, g):
    m, w = y.shape
    tm = _pick_tile(m, (512, 256, 128, 64))
    return pl.pallas_call(
        _s5_post_kernel,
        out_shape=jax.ShapeDtypeStruct((m, w), BF16),
        grid=(m // tm,),
        in_specs=[pl.BlockSpec((tm, w), lambda i: (i, 0)), pl.BlockSpec((w, w), lambda i: (0, 0)),
                  pl.BlockSpec((1, w), lambda i: (0, 0))],
        out_specs=pl.BlockSpec((tm, w), lambda i: (i, 0)),
        compiler_params=_params(("parallel",)),
        name="s5_gelu_glu_norm",
    )(y, w_glu, g.reshape(1, w))


def _s5_discretise(lam_re, lam_im, log_dt, b_re, b_im, c_re, c_im):
    g_cnt = lam_re.shape[0]
    dt = jnp.exp(log_dt.astype(F32))[:, None]
    mag = jnp.exp(lam_re * dt)
    ang = lam_im * dt
    a_re, a_im = mag * jnp.cos(ang), mag * jnp.sin(ang)
    den = lam_re * lam_re + lam_im * lam_im
    num_re = a_re - 1.0
    coef_re = (num_re * lam_re + a_im * lam_im) / den
    coef_im = (a_im * lam_re - num_re * lam_im) / den
    bb_re = coef_re[..., None] * b_re - coef_im[..., None] * b_im
    bb_im = coef_re[..., None] * b_im + coef_im[..., None] * b_re
    nb = g_cnt // S5_GROUPS_PER_BLOCK
    gb = S5_GROUPS_PER_BLOCK
    eye = jnp.eye(gb, dtype=F32)

    def block_diag_in(w):
        w = w.reshape(nb, gb, SSM_P, SSM_H)
        return jnp.einsum('ngph,gk->nghkp', w, eye).reshape(nb, gb * SSM_H, gb * SSM_P)

    def block_diag_out(w):
        w = w.reshape(nb, gb, SSM_H, SSM_P)
        return jnp.einsum('nghp,gk->ngpkh', w, eye).reshape(nb, gb * SSM_P, gb * SSM_H)

    bb = jnp.concatenate([block_diag_in(bb_re), block_diag_in(bb_im)], axis=2).astype(BF16)
    cc = jnp.concatenate([block_diag_out(c_re.astype(F32)), block_diag_out(-c_im.astype(F32))], axis=1).astype(BF16)
    a_re_b = a_re.reshape(nb, 1, gb * SSM_P)
    a_im_b = a_im.reshape(nb, 1, gb * SSM_P)
    return bb, cc, a_re_b, a_im_b


def _s5_mixer(z, b, l, col0, w_ssm, prm, h0_re, h0_im):
    nb = w_ssm // LANES
    gb = S5_GROUPS_PER_BLOCK
    u = z[:, col0:col0 + w_ssm].reshape(b, l, w_ssm).transpose(1, 0, 2)
    u = jnp.pad(u, ((0, 0), (0, SUBLANES - b), (0, 0))).reshape(l * SUBLANES, w_ssm)

    def pack_state(h):
        h = jnp.pad(h.astype(F32), ((0, SUBLANES - b), (0, 0), (0, 0)))
        return h.reshape(SUBLANES, nb, gb * SSM_P).transpose(1, 0, 2)

    h0 = jnp.concatenate([pack_state(h0_re), pack_state(h0_im)], axis=2)
    y, xfin = _s5(u, l, prm['bb'], prm['cc'], prm['d'], prm['a_re'], prm['a_im'], h0)
    o = _s5_post(y, prm['w_glu'], prm['norm_g'])
    o = o.reshape(l, SUBLANES, w_ssm)[:, :b].transpose(1, 0, 2).reshape(b * l, w_ssm)

    def unpack_state(x):
        return x.transpose(1, 0, 2).reshape(SUBLANES, nb * gb, SSM_P)[:b]

    return o, unpack_state(xfin[:, :, :S5_STATE_COLS]), unpack_state(xfin[:, :, S5_STATE_COLS:])


def _att_rope_tables(pos):
    freqs = ROPE_THETA ** (-jnp.arange(0, DK_HALF, 2, dtype=F32) / DK_HALF)
    ang = pos.astype(F32)[:, None] * freqs[None, :]
    c, s = jnp.cos(ang), jnp.sin(ang)
    zero = jnp.zeros_like(s)
    return (jnp.concatenate([c, c, c, c], axis=-1),
            jnp.concatenate([-s, zero, -s, zero], axis=-1),
            jnp.concatenate([zero, s, zero, s], axis=-1))


def _ret_rope_tables(pos):
    freqs = 1.0 / (ROPE_THETA ** jnp.linspace(0.0, 1.0, HEAD_DIM // 2, dtype=F32))
    ang = pos.astype(F32)[:, None] * freqs[None, :]
    c, s = jnp.cos(ang), jnp.sin(ang)
    return jnp.concatenate([c, c], axis=-1), jnp.concatenate([-s, s], axis=-1)


def _row128(v):
    return jnp.tile(v.astype(F32), HEAD_DIM // v.shape[0]).reshape(1, HEAD_DIM)


def _mixers(z, b, l, lw, attend, ret_tabs, ret_s0, s5_h0_re, s5_h0_im):
    d = lw['d_model']
    w_att = lw['w_att']
    o_att, new_k, new_v = attend(z)
    o_ret, s_ret = _retention(z, b, l, 3 * w_att // HEAD_DIM, w_att // HEAD_DIM, ret_tabs, lw['log_gamma'], ret_s0)
    o_ssm, s5_re, s5_im = _s5_mixer(z, b, l, 7 * w_att, d - 2 * w_att, lw['s5'], s5_h0_re, s5_h0_im)
    return [o_att, o_ret.astype(BF16), o_ssm], (new_k, new_v, s_ret, s5_re, s5_im)


def kernel(x_prompt, x_sample, cache_k, cache_v, page_table, state_ret, state_s5_re, state_s5_im,
           norm_mix_g, w_in, q_norm_g, k_norm_g, lambda_q1, lambda_k1, lambda_q2, lambda_k2, subln_g,
           s5_lambda_re, s5_lambda_im, s5_log_dt, s5_b_re, s5_b_im, s5_c_re, s5_c_im, s5_d, s5_w_glu,
           s5_norm_g, w_out, norm_ffn_g, w_gate, w_up, w_down):
    b_p, seq, d = x_prompt.shape
    b_s, n_new, _ = x_sample.shape
    depth = w_in.shape[0]
    w_att = (3 * d) // (8 * HEAD_DIM) * HEAD_DIM
    h_att = w_att // HEAD_DIM
    w_ssm = d - 2 * w_att
    n_pages = page_table.shape[1]
    past = n_pages * PAGE_SIZE
    assert (2 * w_att) % w_ssm == 0 and w_ssm % LANES == 0 and b_p <= SUBLANES and b_s <= SUBLANES
    ssm_rb = (2 * w_att) // w_ssm

    pos_p = jnp.arange(seq, dtype=jnp.int32)
    pos_s = past + jnp.arange(n_new, dtype=jnp.int32)
    att_tabs_p, att_tabs_s = _att_rope_tables(pos_p), _att_rope_tables(pos_s)
    ret_tabs_p, ret_tabs_s = _ret_rope_tables(pos_p), _ret_rope_tables(pos_s)
    log_gamma = jnp.log(1.0 - 2.0 ** (-5.0 - jnp.arange(h_att, dtype=F32)))
    log_gamma_rows = jnp.broadcast_to(log_gamma[:, None, None], (h_att, 1, HEAD_DIM))

    xp = x_prompt.reshape(b_p * seq, d)
    xs = x_sample.reshape(b_s * n_new, d)
    zeros_ret = jnp.zeros((b_p, h_att, HEAD_DIM, HEAD_DIM), F32)
    zeros_s5 = jnp.zeros((b_p, w_ssm // SSM_H, SSM_P), F32)
    outs = [[] for _ in range(10)]
    for layer in range(depth):
        lam_init = 0.8 - 0.6 * math.exp(-0.3 * layer)
        lam = (jnp.exp(jnp.sum(lambda_q1[layer].astype(F32) * lambda_k1[layer].astype(F32)))
               - jnp.exp(jnp.sum(lambda_q2[layer].astype(F32) * lambda_k2[layer].astype(F32))) + lam_init)
        lam_row = jnp.broadcast_to(lam.reshape(1, 1), (1, HEAD_DIM)).astype(F32)
        bb, cc, a_re, a_im = _s5_discretise(
            s5_lambda_re[layer].astype(F32), s5_lambda_im[layer].astype(F32), s5_log_dt[layer],
            s5_b_re[layer].astype(F32), s5_b_im[layer].astype(F32), s5_c_re[layer], s5_c_im[layer])
        lw = {
            'd_model': d, 'w_att': w_att, 'log_gamma': log_gamma_rows,
            's5': {'bb': bb, 'cc': cc, 'a_re': a_re, 'a_im': a_im,
                   'd': s5_d[layer].astype(F32).reshape(1, w_ssm),
                   'w_glu': s5_w_glu[layer].astype(BF16), 'norm_g': s5_norm_g[layer].astype(F32)},
        }
        qg, kg, sg = _row128(q_norm_g[layer]), _row128(k_norm_g[layer]), _row128(subln_g[layer])
        out_scale = 1.0 - lam_init

        hp, hs = _rmsnorm(xp, norm_mix_g[layer]), _rmsnorm(xs, norm_mix_g[layer])
        zp, zs = _matmul_pair([hp], [hs], w_in[layer], [0], tm=1024, tn=512, name="in_proj")

        attend_p = functools.partial(_prompt_attention, b=b_p, l=seq, w_att=w_att, tabs=att_tabs_p,
                                     qg=qg, kg=kg, sg=sg, lam_row=lam_row, out_scale=out_scale)
        mix_p, (kp, vp, rp, p_re, p_im) = _mixers(zp, b_p, seq, lw, attend_p, ret_tabs_p,
                                                  zeros_ret, zeros_s5, zeros_s5)

        def attend_s(z):
            o, nk, nv = _sample_attention(z.reshape(b_s, n_new, z.shape[1]), cache_k, cache_v, layer,
                                          page_table, w_att, att_tabs_s, qg, kg, sg, lam_row, out_scale)
            flat = lambda t: t.reshape(b_s * n_new, w_att)
            return flat(o).astype(BF16), flat(nk), flat(nv)

        mix_s, (k_s, v_s, r_s, s_re, s_im) = _mixers(zs, b_s, n_new, lw, attend_s, ret_tabs_s, state_ret[layer],
                                                     state_s5_re[layer], state_s5_im[layer])

        xp, xs = _matmul_pair(mix_p, mix_s, w_out[layer], [0, 1, ssm_rb], mode="res", res_p=xp, res_s=xs,
                              tm=1024, tn=512, name="out_proj")
        hp, hs = _rmsnorm(xp, norm_ffn_g[layer]), _rmsnorm(xs, norm_ffn_g[layer])
        act_p, act_s = _matmul_pair([hp], [hs], w_gate[layer], [0], mode="swiglu", w2=w_up[layer],
                                    out_dtype=BF16, tm=1024, tn=256, name="ffn_gate_up")
        w_down_b = w_down[layer].astype(BF16)
        xp = _matmul([act_p], w_down_b, [0], mode="res", res=xp, tm=512, tn=256, name="ffn_down")
        xs = _matmul([act_s], w_down_b, [0], mode="res", res=xs, tm=512, tn=256, name="ffn_down")

        for lst, val in zip(outs, (kp.reshape(b_p, seq, h_att, HEAD_DIM), vp.reshape(b_p, seq, h_att, HEAD_DIM),
                                   k_s.reshape(b_s, n_new, h_att, HEAD_DIM), v_s.reshape(b_s, n_new, h_att, HEAD_DIM),
                                   rp, r_s, p_re, p_im, s_re, s_im)):
            lst.append(val)
    return (xp.reshape(b_p, seq, d), xs.reshape(b_s, n_new, d)) + tuple(jnp.stack(o) for o in outs)
```

```python
import functools
import math

import jax
import jax.numpy as jnp
from jax import lax
from jax.experimental import pallas as pl
from jax.experimental.pallas import tpu as pltpu

F32 = jnp.float32
BF16 = jnp.bfloat16

HEAD_DIM = 128
DK_HALF = HEAD_DIM // 2
SSM_H = 16
SSM_P = 64
PAGE_SIZE = 128
RET_CHUNK = 128
ROPE_THETA = 10000.0
EPS = 1e-6
LANES = 128
SUBLANES = 8
S5_GROUPS_PER_BLOCK = LANES // SSM_H
S5_STATE_COLS = S5_GROUPS_PER_BLOCK * SSM_P
NEG_BIG = -1e30
V7X_VMEM_LIMIT_BYTES = 56 * 1024 * 1024


def _pick_tile(n, candidates):
    for c in candidates:
        if n % c == 0:
            return c
    return n


def _params(semantics):
    return pltpu.CompilerParams(dimension_semantics=semantics, vmem_limit_bytes=V7X_VMEM_LIMIT_BYTES)


def _sigmoid(x):
    return 1.0 / (1.0 + jnp.exp(-x))


def _rmsnorm_kernel(x_ref, g_ref, o_ref):
    x = x_ref[...]
    ms = jnp.mean(x * x, axis=-1, keepdims=True)
    o_ref[...] = (x * lax.rsqrt(ms + EPS) * g_ref[...]).astype(o_ref.dtype)


def _rmsnorm(x, g):
    m, d = x.shape
    tm = _pick_tile(m, (512, 256, 128, 64))
    return pl.pallas_call(
        _rmsnorm_kernel,
        out_shape=jax.ShapeDtypeStruct((m, d), BF16),
        grid=(m // tm,),
        in_specs=[pl.BlockSpec((tm, d), lambda i: (i, 0)), pl.BlockSpec((1, d), lambda i: (0, 0))],
        out_specs=pl.BlockSpec((tm, d), lambda i: (i, 0)),
        compiler_params=_params(("parallel",)),
        name="rmsnorm",
    )(x, g.reshape(1, d))


def _mm_res_kernel(a_ref, b_ref, res_ref, o_ref):
    o_ref[...] = res_ref[...] + jnp.dot(a_ref[...], b_ref[...], preferred_element_type=F32)


def _matmul_res(a, w_all, layer, res, *, tm, tn, name):
    m, k = a.shape
    n = w_all.shape[2]
    tm = _pick_tile(m, (tm, 512, 256, 128, 64))
    tn = _pick_tile(n, (tn, 512, 256, 128))
    return pl.pallas_call(
        _mm_res_kernel,
        out_shape=jax.ShapeDtypeStruct((m, n), F32),
        grid=(m // tm, n // tn),
        in_specs=[pl.BlockSpec((tm, k), lambda i, j: (i, 0)),
                  pl.BlockSpec((None, k, tn), lambda i, j: (layer, 0, j)),
                  pl.BlockSpec((tm, tn), lambda i, j: (i, j))],
        out_specs=pl.BlockSpec((tm, tn), lambda i, j: (i, j)),
        compiler_params=_params(("parallel", "arbitrary")),
        name=name,
    )(a, w_all, res)


def _mm_pair_kernel(*refs, n_a, n_w, mode):
    ap_refs, as_refs = refs[:n_a], refs[n_a:2 * n_a]
    w_refs = refs[2 * n_a:2 * n_a + n_w]
    pos = 2 * n_a + n_w
    res_refs = refs[pos:pos + 2] if mode == "res" else ()
    pos += len(res_refs)
    op_ref, os_ref = refs[pos], refs[pos + 1]
    wb_refs = refs[pos + 2:]
    first = pl.program_id(1) == 0

    @pl.when(first)
    def _():
        for w_ref, wb_ref in zip(w_refs, wb_refs):
            wb_ref[...] = w_ref[...].astype(BF16)

    def product(a_refs, res_ref):
        if mode == "swiglu":
            a = a_refs[0][...]
            g = jnp.dot(a, wb_refs[0][...], preferred_element_type=F32)
            u = jnp.dot(a, wb_refs[1][...], preferred_element_type=F32)
            return g * _sigmoid(g) * u
        acc = jnp.dot(a_refs[0][...], wb_refs[0][...], preferred_element_type=F32)
        for a_ref, wb_ref in zip(a_refs[1:], wb_refs[1:]):
            acc = acc + jnp.dot(a_ref[...], wb_ref[...], preferred_element_type=F32)
        return acc if res_ref is None else res_ref[...] + acc

    op_ref[...] = product(ap_refs, res_refs[0] if res_refs else None).astype(op_ref.dtype)

    @pl.when(first)
    def _():
        os_ref[...] = product(as_refs, res_refs[1] if res_refs else None).astype(os_ref.dtype)


def _matmul_pair(ap_list, as_list, w, layer, row_blocks, *, mode="plain", res_p=None, res_s=None, w2=None,
                 out_dtype=F32, tm, tn, name):
    mp, ms = ap_list[0].shape[0], as_list[0].shape[0]
    n = w.shape[2]
    tm = _pick_tile(mp, (tm, 512, 256, 128, 64))
    tn = _pick_tile(n, (tn, 512, 256, 128))
    in_specs = [pl.BlockSpec((tm, a.shape[1]), lambda j, i: (i, 0)) for a in ap_list]
    in_specs += [pl.BlockSpec((ms, a.shape[1]), lambda j, i: (0, 0)) for a in as_list]
    args = list(ap_list) + list(as_list)
    scratch = []
    weights = [(w, a.shape[1], rb) for a, rb in zip(ap_list, row_blocks)]
    if mode == "swiglu":
        weights.append((w2, ap_list[0].shape[1], 0))
    for wt, rows, rb in weights:
        in_specs.append(pl.BlockSpec((None, rows, tn), functools.partial(lambda j, i, rb: (layer, rb, j), rb=rb)))
        args.append(wt)
        scratch.append(pltpu.VMEM((rows, tn), BF16))
    if mode == "res":
        in_specs += [pl.BlockSpec((tm, tn), lambda j, i: (i, j)), pl.BlockSpec((ms, tn), lambda j, i: (0, j))]
        args += [res_p, res_s]
    return pl.pallas_call(
        functools.partial(_mm_pair_kernel, n_a=len(ap_list), n_w=len(weights), mode=mode),
        out_shape=(jax.ShapeDtypeStruct((mp, n), out_dtype), jax.ShapeDtypeStruct((ms, n), out_dtype)),
        grid=(n // tn, mp // tm),
        in_specs=in_specs,
        out_specs=(pl.BlockSpec((tm, tn), lambda j, i: (i, j)), pl.BlockSpec((ms, tn), lambda j, i: (0, j))),
        scratch_shapes=scratch,
        compiler_params=_params(("parallel", "arbitrary")),
        name=name,
    )(*args)


def _half_rmsnorm(x, g):
    lane = lax.broadcasted_iota(jnp.int32, x.shape, 1)
    first = lane < DK_HALF
    sq = x * x
    tot = jnp.sum(sq, axis=-1, keepdims=True)
    s1 = jnp.sum(jnp.where(first, sq, 0.0), axis=-1, keepdims=True)
    ms = jnp.where(first, s1, tot - s1) * (1.0 / DK_HALF)
    return x * lax.rsqrt(ms + EPS) * g


def _att_rope(x, cos, sin_a, sin_b):
    return x * cos + pltpu.roll(x, LANES - DK_HALF // 2, 1) * sin_a + pltpu.roll(x, DK_HALF // 2, 1) * sin_b


def _ret_rope(x, cos, sin_signed):
    return x * cos + pltpu.roll(x, DK_HALF, 1) * sin_signed


def _stack_half_masked(q):
    lane = lax.broadcasted_iota(jnp.int32, q.shape, 1)
    first = lane < DK_HALF
    return jnp.concatenate([jnp.where(first, q, 0.0), jnp.where(first, 0.0, q)], axis=0)


def _sub_layer_norm(o, g, out_scale):
    ms = jnp.mean(o * o, axis=-1, keepdims=True)
    return o * lax.rsqrt(ms + EPS) * g * out_scale


def _prompt_attn_kernel(zq_ref, zk_ref, zv_ref, cq_ref, saq_ref, sbq_ref, ck_ref, sak_ref, sbk_ref,
                        qg_ref, kg_ref, sg_ref, lam_ref,
                        o_ref, newk_ref, newv_ref,
                        k_s, vt_s, q2_s, m_s, l_s, acc_s, *, tq, n_tiles, out_scale):
    i = pl.program_id(2)

    @pl.when(i == 0)
    def _():
        k = _att_rope(_half_rmsnorm(zk_ref[...], kg_ref[...]), ck_ref[...], sak_ref[...], sbk_ref[...])
        newk_ref[...] = k
        k_s[...] = k.astype(BF16)
        v = zv_ref[...]
        newv_ref[...] = v
        for t in range(n_tiles):
            vt_s[t] = v[t * tq:(t + 1) * tq].T.astype(BF16)

    q = _att_rope(_half_rmsnorm(zq_ref[...], qg_ref[...]), cq_ref[...], saq_ref[...], sbq_ref[...])
    q2_s[...] = _stack_half_masked(q * (DK_HALF ** -0.5)).astype(BF16)
    m_s[...] = jnp.full(m_s.shape, NEG_BIG, F32)
    l_s[...] = jnp.zeros(l_s.shape, F32)
    acc_s[...] = jnp.zeros(acc_s.shape, F32)

    def scores(j):
        kj = k_s[pl.ds(pl.multiple_of(j * tq, tq), tq), :]
        return lax.dot_general(kj, q2_s[...], (((1,), (1,)), ((), ())), preferred_element_type=F32)

    def update(s, j):
        m_prev = m_s[...]
        m_new = jnp.maximum(m_prev, jnp.max(s, axis=0, keepdims=True))
        alpha = jnp.exp(m_prev - m_new)
        p = jnp.exp(s - m_new)
        l_s[...] = alpha * l_s[...] + jnp.sum(p, axis=0, keepdims=True)
        acc_s[...] = alpha * acc_s[...] + jnp.dot(vt_s[j], p.astype(BF16), preferred_element_type=F32)
        m_s[...] = m_new

    def body(j, carry):
        update(scores(j), j)
        return carry

    lax.fori_loop(0, i, body, 0)
    s = scores(i)
    key = lax.broadcasted_iota(jnp.int32, s.shape, 0)
    qry = lax.broadcasted_iota(jnp.int32, s.shape, 1)
    qry = jnp.where(qry >= tq, qry - tq, qry)
    update(jnp.where(key <= qry, s, NEG_BIG), i)

    o_full = acc_s[...] / l_s[...]
    o_t = o_full[:, :tq] - lam_ref[0:1, 0:1] * o_full[:, tq:]
    o_ref[...] = _sub_layer_norm(o_t.T, sg_ref[...], out_scale).astype(o_ref.dtype)


def _prompt_attention(z, b, l, w_att, tabs, qg, kg, sg, lam_row, out_scale):
    m = z.shape[0]
    h_att = w_att // HEAD_DIM
    tq = _pick_tile(l, (512, 256, 128))
    nq = l // tq
    cos, sa, sb = tabs
    q_spec = pl.BlockSpec((tq, HEAD_DIM), lambda bb, h, i: (bb * nq + i, h))
    k_spec = pl.BlockSpec((l, HEAD_DIM), lambda bb, h, i: (bb, h_att + h))
    v_spec = pl.BlockSpec((l, HEAD_DIM), lambda bb, h, i: (bb, 2 * h_att + h))
    tq_spec = pl.BlockSpec((tq, HEAD_DIM), lambda bb, h, i: (i, 0))
    tk_spec = pl.BlockSpec((l, HEAD_DIM), lambda bb, h, i: (0, 0))
    row_spec = pl.BlockSpec((1, HEAD_DIM), lambda bb, h, i: (0, 0))
    kv_out_spec = pl.BlockSpec((None, None, l, HEAD_DIM), lambda bb, h, i: (bb, h, 0, 0))
    return pl.pallas_call(
        functools.partial(_prompt_attn_kernel, tq=tq, n_tiles=nq, out_scale=out_scale),
        out_shape=(jax.ShapeDtypeStruct((m, w_att), BF16),
                   jax.ShapeDtypeStruct((b, h_att, l, HEAD_DIM), F32),
                   jax.ShapeDtypeStruct((b, h_att, l, HEAD_DIM), F32)),
        grid=(b, h_att, nq),
        in_specs=[q_spec, k_spec, v_spec, tq_spec, tq_spec, tq_spec, tk_spec, tk_spec, tk_spec,
                  row_spec, row_spec, row_spec, row_spec],
        out_specs=(q_spec, kv_out_spec, kv_out_spec),
        scratch_shapes=[pltpu.VMEM((l, HEAD_DIM), BF16), pltpu.VMEM((nq, HEAD_DIM, tq), BF16),
                        pltpu.VMEM((2 * tq, HEAD_DIM), BF16),
                        pltpu.VMEM((1, 2 * tq), F32), pltpu.VMEM((1, 2 * tq), F32),
                        pltpu.VMEM((HEAD_DIM, 2 * tq), F32)],
        compiler_params=_params(("parallel", "parallel", "arbitrary")),
        name="prompt_diff_attention",
    )(z, z, z, cos, sa, sb, cos, sa, sb, qg, kg, sg, lam_row)


def _sample_attn_kernel(*refs, h_att, n_new, pages_per_step, out_scale):
    g_cnt = pages_per_step
    zq_ref, zk_ref, zv_ref = refs[1:4]
    kc_refs = refs[4:4 + g_cnt]
    vc_refs = refs[4 + g_cnt:4 + 2 * g_cnt]
    cos_ref, sa_ref, sb_ref, qg_ref, kg_ref, sg_ref, lam_ref = refs[4 + 2 * g_cnt:11 + 2 * g_cnt]
    o_ref, newk_ref, newv_ref, q2_s, kn_s, vn_s, m_s, l_s, acc_s = refs[11 + 2 * g_cnt:]
    p = pl.program_id(1)
    rows_h = 2 * n_new

    @pl.when(p == 0)
    def _():
        kn_s[...] = jnp.zeros(kn_s.shape, F32)
        vn_s[...] = jnp.zeros(vn_s.shape, F32)
        for h in range(h_att):
            cs = slice(h * HEAD_DIM, (h + 1) * HEAD_DIM)
            q = _att_rope(_half_rmsnorm(zq_ref[0, :, cs], qg_ref[...]), cos_ref[...], sa_ref[...], sb_ref[...])
            q2_s[h * rows_h:(h + 1) * rows_h, :] = _stack_half_masked(q * (DK_HALF ** -0.5)).astype(BF16)
            k = _att_rope(_half_rmsnorm(zk_ref[0, :, cs], kg_ref[...]), cos_ref[...], sa_ref[...], sb_ref[...])
            newk_ref[0, :, cs] = k
            kn_s[0:n_new, cs] = k
        v = zv_ref[0]
        newv_ref[0] = v
        vn_s[0:n_new, :] = v
        m_s[...] = jnp.full(m_s.shape, NEG_BIG, F32)
        l_s[...] = jnp.zeros(l_s.shape, F32)
        acc_s[...] = jnp.zeros(acc_s.shape, F32)

    def attend(k_pages, v_pages, mask):
        s = jnp.concatenate(
            [jnp.concatenate(
                [lax.dot_general(q2_s[h * rows_h:(h + 1) * rows_h, :], kp[h], (((1,), (1,)), ((), ())),
                                 preferred_element_type=F32) for kp in k_pages], axis=1)
             for h in range(h_att)], axis=0)
        if mask is not None:
            s = jnp.where(mask, s, NEG_BIG)
        m_prev = m_s[...]
        m_new = jnp.maximum(m_prev, jnp.max(s, axis=-1, keepdims=True))
        alpha = jnp.exp(m_prev - m_new)
        pr = jnp.exp(s - m_new)
        l_s[...] = alpha * l_s[...] + jnp.sum(pr, axis=-1, keepdims=True)
        pb = pr.astype(BF16)
        pv = []
        for h in range(h_att):
            acc = None
            for g, vp in enumerate(v_pages):
                part = jnp.dot(pb[h * rows_h:(h + 1) * rows_h, g * PAGE_SIZE:(g + 1) * PAGE_SIZE], vp[h],
                               preferred_element_type=F32)
                acc = part if acc is None else acc + part
            pv.append(acc)
        acc_s[...] = alpha * acc_s[...] + jnp.concatenate(pv, axis=0)
        m_s[...] = m_new

    heads = lambda r: [r[h].astype(BF16) for h in range(h_att)]
    attend([heads(r) for r in kc_refs], [heads(r) for r in vc_refs], None)

    @pl.when(p == pl.num_programs(1) - 1)
    def _():
        shape = (h_att * rows_h, PAGE_SIZE)
        q_tok = lax.rem(lax.broadcasted_iota(jnp.int32, shape, 0), n_new)
        key = lax.broadcasted_iota(jnp.int32, shape, 1)
        head_cols = lambda ref: [ref[:, h * HEAD_DIM:(h + 1) * HEAD_DIM].astype(BF16) for h in range(h_att)]
        attend([head_cols(kn_s)], [head_cols(vn_s)], key <= q_tok)
        o_full = acc_s[...] / l_s[...]
        for h in range(h_att):
            o1 = o_full[h * rows_h:h * rows_h + n_new]
            o2 = o_full[h * rows_h + n_new:(h + 1) * rows_h]
            o = o1 - lam_ref[...] * o2
            o_ref[0, :, h * HEAD_DIM:(h + 1) * HEAD_DIM] = _sub_layer_norm(o, sg_ref[...], out_scale)


def _sample_attention(z3, cache_k, cache_v, layer, page_table, w_att, tabs, qg, kg, sg, lam_row, out_scale):
    bsz, n_new, _ = z3.shape
    n_pages = page_table.shape[1]
    h_att = w_att // HEAD_DIM
    g_cnt = _pick_tile(n_pages, (8, 4, 2, 1))
    cos, sa, sb = tabs
    z_spec = lambda c: pl.BlockSpec((1, n_new, w_att), functools.partial(lambda b, p, pt, c: (b, 0, c), c=c))
    cache_specs = [pl.BlockSpec((None, None, h_att, PAGE_SIZE, HEAD_DIM),
                                functools.partial(lambda b, p, pt, g: (layer, pt[b, p * g_cnt + g], 0, 0, 0), g=g))
                   for g in range(g_cnt)]
    tab_spec = pl.BlockSpec((n_new, HEAD_DIM), lambda b, p, pt: (0, 0))
    row_spec = pl.BlockSpec((1, HEAD_DIM), lambda b, p, pt: (0, 0))
    out_spec = pl.BlockSpec((1, n_new, w_att), lambda b, p, pt: (b, 0, 0))
    rows = h_att * 2 * n_new
    grid_spec = pltpu.PrefetchScalarGridSpec(
        num_scalar_prefetch=1,
        grid=(bsz, n_pages // g_cnt),
        in_specs=[z_spec(0), z_spec(1), z_spec(2)] + cache_specs + cache_specs
                 + [tab_spec, tab_spec, tab_spec, row_spec, row_spec, row_spec, row_spec],
        out_specs=(out_spec, out_spec, out_spec),
        scratch_shapes=[pltpu.VMEM((rows, HEAD_DIM), BF16),
                        pltpu.VMEM((PAGE_SIZE, w_att), F32), pltpu.VMEM((PAGE_SIZE, w_att), F32),
                        pltpu.VMEM((rows, 1), F32), pltpu.VMEM((rows, 1), F32),
                        pltpu.VMEM((rows, HEAD_DIM), F32)],
    )
    return pl.pallas_call(
        functools.partial(_sample_attn_kernel, h_att=h_att, n_new=n_new, pages_per_step=g_cnt,
                          out_scale=out_scale),
        out_shape=(jax.ShapeDtypeStruct((bsz, n_new, w_att), F32),) * 3,
        grid_spec=grid_spec,
        compiler_params=_params(("parallel", "arbitrary")),
        name="sample_paged_diff_attention",
    )(page_table, z3, z3, z3, *([cache_k] * g_cnt), *([cache_v] * g_cnt), cos, sa, sb, qg, kg, sg, lam_row)


def _retention_kernel(zq_ref, zk_ref, zv_ref, zg_ref, cos_ref, sin_ref, lg_ref, s0_ref,
                      o_ref, sfin_ref, *pad_s, chunk, n_chunks):
    cp = RET_CHUNK if chunk < RET_CHUNK else chunk
    lg = lg_ref[0]
    ri = lax.broadcasted_iota(jnp.int32, (cp, cp), 0)
    ci = lax.broadcasted_iota(jnp.int32, (cp, cp), 1)
    diff = (ri - ci).astype(F32)
    dec = jnp.where(diff >= 0, jnp.exp(jnp.maximum(diff, 0.0) * lg), 0.0)
    idx = lax.broadcasted_iota(jnp.int32, (cp, HEAD_DIM), 0).astype(F32)
    cross = jnp.exp((idx + 1.0) * lg)
    kdec_w = jnp.exp((chunk - 1.0 - idx) * lg)
    chunk_decay = jnp.exp(chunk * lg)

    def load(ref, r0):
        x = ref[pl.ds(r0, chunk), :]
        if cp == chunk:
            return x
        buf = pad_s[0]
        buf[...] = jnp.zeros(buf.shape, F32)
        buf[0:chunk, :] = x
        return buf[...]

    def body(c, state):
        r0 = pl.multiple_of(c * chunk, chunk)
        cos = load(cos_ref, r0)
        sin = load(sin_ref, r0)
        q = _ret_rope(load(zq_ref, r0), cos, sin)
        k = _ret_rope(load(zk_ref, r0), cos, sin) * (HEAD_DIM ** -0.5)
        v = load(zv_ref, r0)
        qb, kb, vb = q.astype(BF16), k.astype(BF16), v.astype(BF16)
        s = lax.dot_general(qb, kb, (((1,), (1,)), ((), ())), preferred_element_type=F32) * dec
        o = jnp.dot(s.astype(BF16), vb, preferred_element_type=F32)
        o = o + jnp.dot(qb, state.astype(BF16), preferred_element_type=F32) * cross
        kd = (k * kdec_w).astype(BF16)
        new_state = state * chunk_decay + lax.dot_general(kd, vb, (((0,), (0,)), ((), ())),
                                                          preferred_element_type=F32)
        ms = jnp.mean(o * o, axis=-1, keepdims=True)
        g = load(zg_ref, r0)
        out = o * lax.rsqrt(ms + EPS) * (g * _sigmoid(g))
        o_ref[pl.ds(r0, chunk), :] = out[0:chunk].astype(o_ref.dtype)
        return new_state

    sfin_ref[0, 0] = lax.fori_loop(0, n_chunks, body, s0_ref[0, 0])


def _retention(z, b, l, col0, h_ret, tabs, log_gamma_rows, s0):
    m = z.shape[0]
    chunk = RET_CHUNK if l % RET_CHUNK == 0 else l
    n_chunks = l // chunk
    cos, sin = tabs
    zs = lambda sec: pl.BlockSpec((l, HEAD_DIM), functools.partial(
        lambda bb, h, sec: (bb, col0 + sec * h_ret + h), sec=sec))
    tab_spec = pl.BlockSpec((l, HEAD_DIM), lambda bb, h: (0, 0))
    scratch = [] if chunk == RET_CHUNK else [pltpu.VMEM((RET_CHUNK, HEAD_DIM), F32)]
    return pl.pallas_call(
        functools.partial(_retention_kernel, chunk=chunk, n_chunks=n_chunks),
        out_shape=(jax.ShapeDtypeStruct((m, h_ret * HEAD_DIM), F32 if chunk < 16 else BF16),
                   jax.ShapeDtypeStruct((b, h_ret, HEAD_DIM, HEAD_DIM), F32)),
        grid=(b, h_ret),
        in_specs=[zs(0), zs(1), zs(2), zs(3), tab_spec, tab_spec,
                  pl.BlockSpec((1, 1, HEAD_DIM), lambda bb, h: (h, 0, 0)),
                  pl.BlockSpec((1, 1, HEAD_DIM, HEAD_DIM), lambda bb, h: (bb, h, 0, 0))],
        out_specs=(pl.BlockSpec((l, HEAD_DIM), lambda bb, h: (bb, h)),
                   pl.BlockSpec((1, 1, HEAD_DIM, HEAD_DIM), lambda bb, h: (bb, h, 0, 0))),
        scratch_shapes=scratch,
        compiler_params=_params(("parallel", "parallel")),
        name="retention",
    )(z, z, z, z, cos, sin, log_gamma_rows, s0)


def _s5_kernel(u_ref, bb_ref, cc_ref, d_ref, are_ref, aim_ref, h0_ref, y_ref, xfin_ref, bu_s, x_s, *, steps):
    i = pl.program_id(1)
    nc = S5_STATE_COLS

    @pl.when(i == 0)
    def _():
        x_s[...] = h0_ref[0]

    u = u_ref[...]
    bu_s[...] = jnp.dot(u.astype(BF16), bb_ref[0], preferred_element_type=F32)
    a_re = jnp.broadcast_to(are_ref[0], (SUBLANES, nc))
    a_im = jnp.broadcast_to(aim_ref[0], (SUBLANES, nc))

    def body(t, carry):
        xr, xi = carry
        r0 = pl.multiple_of(t * SUBLANES, SUBLANES)
        nr = a_re * xr - a_im * xi + bu_s[pl.ds(r0, SUBLANES), 0:nc]
        ni = a_re * xi + a_im * xr + bu_s[pl.ds(r0, SUBLANES), nc:2 * nc]
        bu_s[pl.ds(r0, SUBLANES), 0:nc] = nr
        bu_s[pl.ds(r0, SUBLANES), nc:2 * nc] = ni
        return nr, ni

    xr, xi = lax.fori_loop(0, steps, body, (x_s[:, 0:nc], x_s[:, nc:2 * nc]), unroll=min(8, steps))
    x_s[:, 0:nc] = xr
    x_s[:, nc:2 * nc] = xi
    y_ref[...] = jnp.dot(bu_s[...].astype(BF16), cc_ref[0], preferred_element_type=F32) + d_ref[...] * u

    @pl.when(i == pl.num_programs(1) - 1)
    def _():
        xfin_ref[0] = x_s[...]


def _s5(u_rows, n_steps, bb, cc, d_row, a_re, a_im, h0):
    rows, w_ssm = u_rows.shape
    n_blocks = w_ssm // LANES
    t_chunk = _pick_tile(n_steps, (256, 128, 64, 32, 16, 8))
    r_chunk = t_chunk * SUBLANES
    return pl.pallas_call(
        functools.partial(_s5_kernel, steps=t_chunk),
        out_shape=(jax.ShapeDtypeStruct((rows, w_ssm), F32),
                   jax.ShapeDtypeStruct((n_blocks, SUBLANES, 2 * S5_STATE_COLS), F32)),
        grid=(n_blocks, n_steps // t_chunk),
        in_specs=[pl.BlockSpec((r_chunk, LANES), lambda j, i: (i, j)),
                  pl.BlockSpec((1, LANES, 2 * S5_STATE_COLS), lambda j, i: (j, 0, 0)),
                  pl.BlockSpec((1, 2 * S5_STATE_COLS, LANES), lambda j, i: (j, 0, 0)),
                  pl.BlockSpec((1, LANES), lambda j, i: (0, j)),
                  pl.BlockSpec((1, 1, S5_STATE_COLS), lambda j, i: (j, 0, 0)),
                  pl.BlockSpec((1, 1, S5_STATE_COLS), lambda j, i: (j, 0, 0)),
                  pl.BlockSpec((1, SUBLANES, 2 * S5_STATE_COLS), lambda j, i: (j, 0, 0))],
        out_specs=(pl.BlockSpec((r_chunk, LANES), lambda j, i: (i, j)),
                   pl.BlockSpec((1, SUBLANES, 2 * S5_STATE_COLS), lambda j, i: (j, 0, 0))),
        scratch_shapes=[pltpu.VMEM((r_chunk, 2 * S5_STATE_COLS), F32),
                        pltpu.VMEM((SUBLANES, 2 * S5_STATE_COLS), F32)],
        compiler_params=_params(("parallel", "arbitrary")),
        name="s5_scan",
    )(u_rows, bb, cc, d_row, a_re, a_im, h0)


def _s5_post_kernel(y_ref, w_ref, g_ref, o_ref):
    y = y_ref[...]
    y = 0.5 * y * (1.0 + jnp.tanh(math.sqrt(2.0 / math.pi) * (y + 0.044715 * (y * y * y))))
    gate = _sigmoid(jnp.dot(y.astype(BF16), w_ref[...], preferred_element_type=F32))
    y = y * gate
    ms = jnp.mean(y * y, axis=-1, keepdims=True)
    o_ref[...] = (y * lax.rsqrt(ms + EPS) * g_ref[...]).astype(o_ref.dtype)


def _s5_post(y, w_glu, g):
    m, w = y.shape
    tm = _pick_tile(m, (512, 256, 128, 64))
    return pl.pallas_call(
        _s5_post_kernel,
        out_shape=jax.ShapeDtypeStruct((m, w), BF16),
        grid=(m // tm,),
        in_specs=[pl.BlockSpec((tm, w), lambda i: (i, 0)), pl.BlockSpec((w, w), lambda i: (0, 0)),
                  pl.BlockSpec((1, w), lambda i: (0, 0))],
        out_specs=pl.BlockSpec((tm, w), lambda i: (i, 0)),
        compiler_params=_params(("parallel",)),
        name="s5_gelu_glu_norm",
    )(y, w_glu, g.reshape(1, w))


def _s5_discretise(lam_re, lam_im, log_dt, b_re, b_im, c_re, c_im):
    g_cnt = lam_re.shape[0]
    dt = jnp.exp(log_dt.astype(F32))[:, None]
    mag = jnp.exp(lam_re * dt)
    ang = lam_im * dt
    a_re, a_im = mag * jnp.cos(ang), mag * jnp.sin(ang)
    den = lam_re * lam_re + lam_im * lam_im
    num_re = a_re - 1.0
    coef_re = (num_re * lam_re + a_im * lam_im) / den
    coef_im = (a_im * lam_re - num_re * lam_im) / den
    bb_re = coef_re[..., None] * b_re - coef_im[..., None] * b_im
    bb_im = coef_re[..., None] * b_im + coef_im[..., None] * b_re
    nb = g_cnt // S5_GROUPS_PER_BLOCK
    gb = S5_GROUPS_PER_BLOCK
    eye = jnp.eye(gb, dtype=F32)

    def block_diag_in(w):
        w = w.reshape(nb, gb, SSM_P, SSM_H)
        return jnp.einsum('ngph,gk->nghkp', w, eye).reshape(nb, gb * SSM_H, gb * SSM_P)

    def block_diag_out(w):
        w = w.reshape(nb, gb, SSM_H, SSM_P)
        return jnp.einsum('nghp,gk->ngpkh', w, eye).reshape(nb, gb * SSM_P, gb * SSM_H)

    bb = jnp.concatenate([block_diag_in(bb_re), block_diag_in(bb_im)], axis=2).astype(BF16)
    cc = jnp.concatenate([block_diag_out(c_re.astype(F32)), block_diag_out(-c_im.astype(F32))], axis=1).astype(BF16)
    a_re_b = a_re.reshape(nb, 1, gb * SSM_P)
    a_im_b = a_im.reshape(nb, 1, gb * SSM_P)
    return bb, cc, a_re_b, a_im_b


def _s5_mixer(z, b, l, col0, w_ssm, prm, h0_re, h0_im):
    nb = w_ssm // LANES
    gb = S5_GROUPS_PER_BLOCK
    u = z[:, col0:col0 + w_ssm].reshape(b, l, w_ssm).transpose(1, 0, 2)
    u = jnp.pad(u, ((0, 0), (0, SUBLANES - b), (0, 0))).reshape(l * SUBLANES, w_ssm)

    def pack_state(h):
        h = jnp.pad(h.astype(F32), ((0, SUBLANES - b), (0, 0), (0, 0)))
        return h.reshape(SUBLANES, nb, gb * SSM_P).transpose(1, 0, 2)

    h0 = jnp.concatenate([pack_state(h0_re), pack_state(h0_im)], axis=2)
    y, xfin = _s5(u, l, prm['bb'], prm['cc'], prm['d'], prm['a_re'], prm['a_im'], h0)
    o = _s5_post(y, prm['w_glu'], prm['norm_g'])
    o = o.reshape(l, SUBLANES, w_ssm)[:, :b].transpose(1, 0, 2).reshape(b * l, w_ssm)

    def unpack_state(x):
        return x.transpose(1, 0, 2).reshape(SUBLANES, nb * gb, SSM_P)[:b]

    return o, unpack_state(xfin[:, :, :S5_STATE_COLS]), unpack_state(xfin[:, :, S5_STATE_COLS:])


def _att_rope_tables(pos):
    freqs = ROPE_THETA ** (-jnp.arange(0, DK_HALF, 2, dtype=F32) / DK_HALF)
    ang = pos.astype(F32)[:, None] * freqs[None, :]
    c, s = jnp.cos(ang), jnp.sin(ang)
    zero = jnp.zeros_like(s)
    return (jnp.concatenate([c, c, c, c], axis=-1),
            jnp.concatenate([-s, zero, -s, zero], axis=-1),
            jnp.concatenate([zero, s, zero, s], axis=-1))


def _ret_rope_tables(pos):
    freqs = 1.0 / (ROPE_THETA ** jnp.linspace(0.0, 1.0, HEAD_DIM // 2, dtype=F32))
    ang = pos.astype(F32)[:, None] * freqs[None, :]
    c, s = jnp.cos(ang), jnp.sin(ang)
    return jnp.concatenate([c, c], axis=-1), jnp.concatenate([-s, s], axis=-1)


def _row128(v):
    return jnp.tile(v.astype(F32), HEAD_DIM // v.shape[0]).reshape(1, HEAD_DIM)


def _mixers(z, b, l, lw, attend, ret_tabs, ret_s0, s5_h0_re, s5_h0_im):
    d = lw['d_model']
    w_att = lw['w_att']
    o_att, new_k, new_v = attend(z)
    o_ret, s_ret = _retention(z, b, l, 3 * w_att // HEAD_DIM, w_att // HEAD_DIM, ret_tabs, lw['log_gamma'], ret_s0)
    o_ssm, s5_re, s5_im = _s5_mixer(z, b, l, 7 * w_att, d - 2 * w_att, lw['s5'], s5_h0_re, s5_h0_im)
    return [o_att, o_ret.astype(BF16), o_ssm], (new_k, new_v, s_ret, s5_re, s5_im)


def kernel(x_prompt, x_sample, cache_k, cache_v, page_table, state_ret, state_s5_re, state_s5_im,
           norm_mix_g, w_in, q_norm_g, k_norm_g, lambda_q1, lambda_k1, lambda_q2, lambda_k2, subln_g,
           s5_lambda_re, s5_lambda_im, s5_log_dt, s5_b_re, s5_b_im, s5_c_re, s5_c_im, s5_d, s5_w_glu,
           s5_norm_g, w_out, norm_ffn_g, w_gate, w_up, w_down):
    b_p, seq, d = x_prompt.shape
    b_s, n_new, _ = x_sample.shape
    depth = w_in.shape[0]
    w_att = (3 * d) // (8 * HEAD_DIM) * HEAD_DIM
    h_att = w_att // HEAD_DIM
    w_ssm = d - 2 * w_att
    n_pages = page_table.shape[1]
    past = n_pages * PAGE_SIZE
    assert (2 * w_att) % w_ssm == 0 and w_ssm % LANES == 0 and b_p <= SUBLANES and b_s <= SUBLANES
    ssm_rb = (2 * w_att) // w_ssm

    pos_p = jnp.arange(seq, dtype=jnp.int32)
    pos_s = past + jnp.arange(n_new, dtype=jnp.int32)
    att_tabs_p, att_tabs_s = _att_rope_tables(pos_p), _att_rope_tables(pos_s)
    ret_tabs_p, ret_tabs_s = _ret_rope_tables(pos_p), _ret_rope_tables(pos_s)
    log_gamma = jnp.log(1.0 - 2.0 ** (-5.0 - jnp.arange(h_att, dtype=F32)))
    log_gamma_rows = jnp.broadcast_to(log_gamma[:, None, None], (h_att, 1, HEAD_DIM))

    xp = x_prompt.reshape(b_p * seq, d)
    xs = x_sample.reshape(b_s * n_new, d)
    zeros_ret = jnp.zeros((b_p, h_att, HEAD_DIM, HEAD_DIM), F32)
    zeros_s5 = jnp.zeros((b_p, w_ssm // SSM_H, SSM_P), F32)
    w_down_b = w_down.astype(BF16)
    cache_kh = jnp.transpose(cache_k, (0, 1, 3, 2, 4))
    cache_vh = jnp.transpose(cache_v, (0, 1, 3, 2, 4))
    outs = [[] for _ in range(10)]
    for layer in range(depth):
        lam_init = 0.8 - 0.6 * math.exp(-0.3 * layer)
        lam = (jnp.exp(jnp.sum(lambda_q1[layer].astype(F32) * lambda_k1[layer].astype(F32)))
               - jnp.exp(jnp.sum(lambda_q2[layer].astype(F32) * lambda_k2[layer].astype(F32))) + lam_init)
        lam_row = jnp.broadcast_to(lam.reshape(1, 1), (1, HEAD_DIM)).astype(F32)
        bb, cc, a_re, a_im = _s5_discretise(
            s5_lambda_re[layer].astype(F32), s5_lambda_im[layer].astype(F32), s5_log_dt[layer],
            s5_b_re[layer].astype(F32), s5_b_im[layer].astype(F32), s5_c_re[layer], s5_c_im[layer])
        lw = {
            'd_model': d, 'w_att': w_att, 'log_gamma': log_gamma_rows,
            's5': {'bb': bb, 'cc': cc, 'a_re': a_re, 'a_im': a_im,
                   'd': s5_d[layer].astype(F32).reshape(1, w_ssm),
                   'w_glu': s5_w_glu[layer].astype(BF16), 'norm_g': s5_norm_g[layer].astype(F32)},
        }
        qg, kg, sg = _row128(q_norm_g[layer]), _row128(k_norm_g[layer]), _row128(subln_g[layer])
        out_scale = 1.0 - lam_init

        hp, hs = _rmsnorm(xp, norm_mix_g[layer]), _rmsnorm(xs, norm_mix_g[layer])
        zp, zs = _matmul_pair([hp], [hs], w_in, layer, [0], tm=1024, tn=512, name="in_proj")

        attend_p = functools.partial(_prompt_attention, b=b_p, l=seq, w_att=w_att, tabs=att_tabs_p,
                                     qg=qg, kg=kg, sg=sg, lam_row=lam_row, out_scale=out_scale)
        mix_p, (kp, vp, rp, p_re, p_im) = _mixers(zp, b_p, seq, lw, attend_p, ret_tabs_p,
                                                  zeros_ret, zeros_s5, zeros_s5)

        def attend_s(z):
            o, nk, nv = _sample_attention(z.reshape(b_s, n_new, z.shape[1]), cache_kh, cache_vh, layer,
                                          page_table, w_att, att_tabs_s, qg, kg, sg, lam_row, out_scale)
            flat = lambda t: t.reshape(b_s * n_new, w_att)
            return flat(o).astype(BF16), flat(nk), flat(nv)

        mix_s, (k_s, v_s, r_s, s_re, s_im) = _mixers(zs, b_s, n_new, lw, attend_s, ret_tabs_s, state_ret[layer],
                                                     state_s5_re[layer], state_s5_im[layer])

        xp, xs = _matmul_pair(mix_p, mix_s, w_out, layer, [0, 1, ssm_rb], mode="res", res_p=xp, res_s=xs,
                              tm=1024, tn=512, name="out_proj")
        hp, hs = _rmsnorm(xp, norm_ffn_g[layer]), _rmsnorm(xs, norm_ffn_g[layer])
        act_p, act_s = _matmul_pair([hp], [hs], w_gate, layer, [0], mode="swiglu", w2=w_up,
                                    out_dtype=BF16, tm=1024, tn=256, name="ffn_gate_up")
        xp = _matmul_res(act_p, w_down_b, layer, xp, tm=512, tn=256, name="ffn_down")
        xs = _matmul_res(act_s, w_down_b, layer, xs, tm=512, tn=256, name="ffn_down")

        for lst, val in zip(outs, (kp, vp,
                                   k_s.reshape(b_s, n_new, h_att, HEAD_DIM), v_s.reshape(b_s, n_new, h_att, HEAD_DIM),
                                   rp, r_s, p_re, p_im, s_re, s_im)):
            lst.append(val)
    stacked = [jnp.stack(o) for o in outs]
    for idx in (0, 1):
        stacked[idx] = jnp.transpose(stacked[idx], (0, 1, 3, 2, 4))
    return (xp.reshape(b_p, seq, d), xs.reshape(b_s, n_new, d)) + tuple(stacked)
```

```python
import functools
import math

import jax
import jax.numpy as jnp
from jax import lax
from jax.experimental import pallas as pl
from jax.experimental.pallas import tpu as pltpu

F32 = jnp.float32
BF16 = jnp.bfloat16

HEAD_DIM = 128
DK_HALF = HEAD_DIM // 2
SSM_H = 16
SSM_P = 64
PAGE_SIZE = 128
RET_CHUNK = 128
ROPE_THETA = 10000.0
EPS = 1e-6
LANES = 128
SUBLANES = 8
S5_GROUPS_PER_BLOCK = LANES // SSM_H
S5_STATE_COLS = S5_GROUPS_PER_BLOCK * SSM_P
NEG_BIG = -1e30
V7X_VMEM_LIMIT_BYTES = 56 * 1024 * 1024


def _pick_tile(n, candidates):
    for c in candidates:
        if n % c == 0:
            return c
    return n


def _params(semantics):
    return pltpu.CompilerParams(dimension_semantics=semantics, vmem_limit_bytes=V7X_VMEM_LIMIT_BYTES)


def _sigmoid(x):
    return 1.0 / (1.0 + jnp.exp(-x))


def _rmsnorm_kernel(x_ref, g_ref, o_ref):
    x = x_ref[...]
    ms = jnp.mean(x * x, axis=-1, keepdims=True)
    o_ref[...] = (x * lax.rsqrt(ms + EPS) * g_ref[...]).astype(o_ref.dtype)


def _rmsnorm(x, g):
    m, d = x.shape
    tm = _pick_tile(m, (512, 256, 128, 64))
    return pl.pallas_call(
        _rmsnorm_kernel,
        out_shape=jax.ShapeDtypeStruct((m, d), BF16),
        grid=(m // tm,),
        in_specs=[pl.BlockSpec((tm, d), lambda i: (i, 0)), pl.BlockSpec((1, d), lambda i: (0, 0))],
        out_specs=pl.BlockSpec((tm, d), lambda i: (i, 0)),
        compiler_params=_params(("parallel",)),
        name="rmsnorm",
    )(x, g.reshape(1, d))


def _mm_res_kernel(a_ref, b_ref, res_ref, o_ref):
    o_ref[...] = res_ref[...] + jnp.dot(a_ref[...], b_ref[...], preferred_element_type=F32)


def _matmul_res(a, w_all, layer, res, *, tm, tn, name):
    m, k = a.shape
    n = w_all.shape[2]
    tm = _pick_tile(m, (tm, 512, 256, 128, 64))
    tn = _pick_tile(n, (tn, 512, 256, 128))
    return pl.pallas_call(
        _mm_res_kernel,
        out_shape=jax.ShapeDtypeStruct((m, n), F32),
        grid=(m // tm, n // tn),
        in_specs=[pl.BlockSpec((tm, k), lambda i, j: (i, 0)),
                  pl.BlockSpec((None, k, tn), lambda i, j: (layer, 0, j)),
                  pl.BlockSpec((tm, tn), lambda i, j: (i, j))],
        out_specs=pl.BlockSpec((tm, tn), lambda i, j: (i, j)),
        compiler_params=_params(("parallel", "arbitrary")),
        name=name,
    )(a, w_all, res)


def _mm_pair_kernel(*refs, n_a, n_w, mode):
    ap_refs, as_refs = refs[:n_a], refs[n_a:2 * n_a]
    w_refs = refs[2 * n_a:2 * n_a + n_w]
    pos = 2 * n_a + n_w
    res_refs = refs[pos:pos + 2] if mode == "res" else ()
    pos += len(res_refs)
    op_ref, os_ref = refs[pos], refs[pos + 1]
    wb_refs = refs[pos + 2:]
    first = pl.program_id(1) == 0

    @pl.when(first)
    def _():
        for w_ref, wb_ref in zip(w_refs, wb_refs):
            wb_ref[...] = w_ref[...].astype(BF16)

    def product(a_vals):
        if mode == "swiglu":
            g = jnp.dot(a_vals[0], wb_refs[0][...], preferred_element_type=F32)
            u = jnp.dot(a_vals[0], wb_refs[1][...], preferred_element_type=F32)
            return g * _sigmoid(g) * u
        acc = jnp.dot(a_vals[0], wb_refs[0][...], preferred_element_type=F32)
        for a, wb_ref in zip(a_vals[1:], wb_refs[1:]):
            acc = acc + jnp.dot(a, wb_ref[...], preferred_element_type=F32)
        return acc

    def store(o_ref, val, res_ref):
        o_ref[...] = (val if res_ref is None else res_ref[...] + val).astype(o_ref.dtype)

    res_p, res_s = res_refs if res_refs else (None, None)
    tm = op_ref.shape[0]

    @pl.when(first)
    def _():
        both = product([jnp.concatenate([ap[...], as_[...]], axis=0) for ap, as_ in zip(ap_refs, as_refs)])
        store(op_ref, both[:tm], res_p)
        store(os_ref, both[tm:], res_s)

    @pl.when(jnp.logical_not(first))
    def _():
        store(op_ref, product([ap[...] for ap in ap_refs]), res_p)


def _matmul_pair(ap_list, as_list, w, layer, row_blocks, *, mode="plain", res_p=None, res_s=None, w2=None,
                 out_dtype=F32, tm, tn, name):
    mp, ms = ap_list[0].shape[0], as_list[0].shape[0]
    n = w.shape[2]
    tm = _pick_tile(mp, (tm, 512, 256, 128, 64))
    tn = _pick_tile(n, (tn, 512, 256, 128))
    in_specs = [pl.BlockSpec((tm, a.shape[1]), lambda j, i: (i, 0)) for a in ap_list]
    in_specs += [pl.BlockSpec((ms, a.shape[1]), lambda j, i: (0, 0)) for a in as_list]
    args = list(ap_list) + list(as_list)
    scratch = []
    weights = [(w, a.shape[1], rb) for a, rb in zip(ap_list, row_blocks)]
    if mode == "swiglu":
        weights.append((w2, ap_list[0].shape[1], 0))
    for wt, rows, rb in weights:
        in_specs.append(pl.BlockSpec((None, rows, tn), functools.partial(lambda j, i, rb: (layer, rb, j), rb=rb)))
        args.append(wt)
        scratch.append(pltpu.VMEM((rows, tn), BF16))
    if mode == "res":
        in_specs += [pl.BlockSpec((tm, tn), lambda j, i: (i, j)), pl.BlockSpec((ms, tn), lambda j, i: (0, j))]
        args += [res_p, res_s]
    return pl.pallas_call(
        functools.partial(_mm_pair_kernel, n_a=len(ap_list), n_w=len(weights), mode=mode),
        out_shape=(jax.ShapeDtypeStruct((mp, n), out_dtype), jax.ShapeDtypeStruct((ms, n), out_dtype)),
        grid=(n // tn, mp // tm),
        in_specs=in_specs,
        out_specs=(pl.BlockSpec((tm, tn), lambda j, i: (i, j)), pl.BlockSpec((ms, tn), lambda j, i: (0, j))),
        scratch_shapes=scratch,
        compiler_params=_params(("parallel", "arbitrary")),
        name=name,
    )(*args)


def _half_rmsnorm(x, g):
    lane = lax.broadcasted_iota(jnp.int32, x.shape, 1)
    first = lane < DK_HALF
    sq = x * x
    tot = jnp.sum(sq, axis=-1, keepdims=True)
    s1 = jnp.sum(jnp.where(first, sq, 0.0), axis=-1, keepdims=True)
    ms = jnp.where(first, s1, tot - s1) * (1.0 / DK_HALF)
    return x * lax.rsqrt(ms + EPS) * g


def _att_rope(x, cos, sin_a, sin_b):
    return x * cos + pltpu.roll(x, LANES - DK_HALF // 2, 1) * sin_a + pltpu.roll(x, DK_HALF // 2, 1) * sin_b


def _ret_rope(x, cos, sin_signed):
    return x * cos + pltpu.roll(x, DK_HALF, 1) * sin_signed


def _stack_half_masked(q):
    lane = lax.broadcasted_iota(jnp.int32, q.shape, 1)
    first = lane < DK_HALF
    return jnp.concatenate([jnp.where(first, q, 0.0), jnp.where(first, 0.0, q)], axis=0)


def _sub_layer_norm(o, g, out_scale):
    ms = jnp.mean(o * o, axis=-1, keepdims=True)
    return o * lax.rsqrt(ms + EPS) * g * out_scale


def _prompt_attn_kernel(zq_ref, zk_ref, zv_ref, cq_ref, saq_ref, sbq_ref, ck_ref, sak_ref, sbk_ref,
                        qg_ref, kg_ref, sg_ref, lam_ref,
                        o_ref, newk_ref, newv_ref,
                        k_s, vt_s, q2_s, m_s, l_s, acc_s, *, tq, n_tiles, n_chains, out_scale):
    i = pl.program_id(2)

    @pl.when(i == 0)
    def _():
        k = _att_rope(_half_rmsnorm(zk_ref[...], kg_ref[...]), ck_ref[...], sak_ref[...], sbk_ref[...])
        newk_ref[...] = k
        k_s[...] = k.astype(BF16)
        v = zv_ref[...]
        newv_ref[...] = v
        for t in range(n_tiles):
            vt_s[t] = v[t * tq:(t + 1) * tq].T.astype(BF16)

    q = _att_rope(_half_rmsnorm(zq_ref[...], qg_ref[...]), cq_ref[...], saq_ref[...], sbq_ref[...])
    q2_s[...] = _stack_half_masked(q * (DK_HALF ** -0.5)).astype(BF16)
    m_s[...] = jnp.full(m_s.shape, NEG_BIG, F32)
    l_s[...] = jnp.zeros(l_s.shape, F32)
    acc_s[...] = jnp.zeros(acc_s.shape, F32)

    wc = (2 * tq) // n_chains

    def block(j, masked):
        kj = k_s[pl.ds(pl.multiple_of(j * tq, tq), tq), :]
        vj = vt_s[j]
        for c in range(n_chains):
            cols = slice(c * wc, (c + 1) * wc)
            s = lax.dot_general(kj, q2_s[cols, :], (((1,), (1,)), ((), ())), preferred_element_type=F32)
            if masked:
                key = lax.broadcasted_iota(jnp.int32, s.shape, 0)
                qry = lax.broadcasted_iota(jnp.int32, s.shape, 1) + c * wc
                qry = jnp.where(qry >= tq, qry - tq, qry)
                s = jnp.where(key <= qry, s, NEG_BIG)
            m_prev = m_s[:, cols]
            m_new = jnp.maximum(m_prev, jnp.max(s, axis=0, keepdims=True))
            alpha = jnp.exp(m_prev - m_new)
            p = jnp.exp(s - m_new)
            l_s[:, cols] = alpha * l_s[:, cols] + jnp.sum(p, axis=0, keepdims=True)
            acc_s[:, cols] = alpha * acc_s[:, cols] + jnp.dot(vj, p.astype(BF16), preferred_element_type=F32)
            m_s[:, cols] = m_new

    def body(j, carry):
        block(j, False)
        return carry

    lax.fori_loop(0, i, body, 0)
    block(i, True)

    o_full = acc_s[...] / l_s[...]
    o_t = o_full[:, :tq] - lam_ref[0:1, 0:1] * o_full[:, tq:]
    o_ref[...] = _sub_layer_norm(o_t.T, sg_ref[...], out_scale).astype(o_ref.dtype)


def _prompt_attention(z, b, l, w_att, tabs, qg, kg, sg, lam_row, out_scale):
    m = z.shape[0]
    h_att = w_att // HEAD_DIM
    tq = _pick_tile(l, (512, 256, 128))
    nq = l // tq
    cos, sa, sb = tabs
    q_spec = pl.BlockSpec((tq, HEAD_DIM), lambda bb, h, i: (bb * nq + i, h))
    k_spec = pl.BlockSpec((l, HEAD_DIM), lambda bb, h, i: (bb, h_att + h))
    v_spec = pl.BlockSpec((l, HEAD_DIM), lambda bb, h, i: (bb, 2 * h_att + h))
    tq_spec = pl.BlockSpec((tq, HEAD_DIM), lambda bb, h, i: (i, 0))
    tk_spec = pl.BlockSpec((l, HEAD_DIM), lambda bb, h, i: (0, 0))
    row_spec = pl.BlockSpec((1, HEAD_DIM), lambda bb, h, i: (0, 0))
    kv_out_spec = pl.BlockSpec((None, None, l, HEAD_DIM), lambda bb, h, i: (bb, h, 0, 0))
    return pl.pallas_call(
        functools.partial(_prompt_attn_kernel, tq=tq, n_tiles=nq, n_chains=1, out_scale=out_scale),
        out_shape=(jax.ShapeDtypeStruct((m, w_att), BF16),
                   jax.ShapeDtypeStruct((b, h_att, l, HEAD_DIM), F32),
                   jax.ShapeDtypeStruct((b, h_att, l, HEAD_DIM), F32)),
        grid=(b, h_att, nq),
        in_specs=[q_spec, k_spec, v_spec, tq_spec, tq_spec, tq_spec, tk_spec, tk_spec, tk_spec,
                  row_spec, row_spec, row_spec, row_spec],
        out_specs=(q_spec, kv_out_spec, kv_out_spec),
        scratch_shapes=[pltpu.VMEM((l, HEAD_DIM), BF16), pltpu.VMEM((nq, HEAD_DIM, tq), BF16),
                        pltpu.VMEM((2 * tq, HEAD_DIM), BF16),
                        pltpu.VMEM((1, 2 * tq), F32), pltpu.VMEM((1, 2 * tq), F32),
                        pltpu.VMEM((HEAD_DIM, 2 * tq), F32)],
        compiler_params=_params(("parallel", "parallel", "arbitrary")),
        name="prompt_diff_attention",
    )(z, z, z, cos, sa, sb, cos, sa, sb, qg, kg, sg, lam_row)


def _sample_attn_kernel(*refs, h_att, n_new, pages_per_step, out_scale):
    g_cnt = pages_per_step
    zq_ref, zk_ref, zv_ref = refs[1:4]
    kc_refs = refs[4:4 + g_cnt]
    vc_refs = refs[4 + g_cnt:4 + 2 * g_cnt]
    cos_ref, sa_ref, sb_ref, qg_ref, kg_ref, sg_ref, lam_ref = refs[4 + 2 * g_cnt:11 + 2 * g_cnt]
    o_ref, newk_ref, newv_ref, q2_s, kn_s, vn_s, m_s, l_s, acc_s = refs[11 + 2 * g_cnt:]
    p = pl.program_id(1)
    rows_h = 2 * n_new

    @pl.when(p == 0)
    def _():
        kn_s[...] = jnp.zeros(kn_s.shape, F32)
        vn_s[...] = jnp.zeros(vn_s.shape, F32)
        for h in range(h_att):
            cs = slice(h * HEAD_DIM, (h + 1) * HEAD_DIM)
            q = _att_rope(_half_rmsnorm(zq_ref[0, :, cs], qg_ref[...]), cos_ref[...], sa_ref[...], sb_ref[...])
            q2_s[h * rows_h:(h + 1) * rows_h, :] = _stack_half_masked(q * (DK_HALF ** -0.5)).astype(BF16)
            k = _att_rope(_half_rmsnorm(zk_ref[0, :, cs], kg_ref[...]), cos_ref[...], sa_ref[...], sb_ref[...])
            newk_ref[0, :, cs] = k
            kn_s[0:n_new, cs] = k
        v = zv_ref[0]
        newv_ref[0] = v
        vn_s[0:n_new, :] = v
        m_s[...] = jnp.full(m_s.shape, NEG_BIG, F32)
        l_s[...] = jnp.zeros(l_s.shape, F32)
        acc_s[...] = jnp.zeros(acc_s.shape, F32)

    def attend(k_pages, v_pages, mask):
        s = jnp.concatenate(
            [jnp.concatenate(
                [lax.dot_general(q2_s[h * rows_h:(h + 1) * rows_h, :], kp[h], (((1,), (1,)), ((), ())),
                                 preferred_element_type=F32) for kp in k_pages], axis=1)
             for h in range(h_att)], axis=0)
        if mask is not None:
            s = jnp.where(mask, s, NEG_BIG)
        m_prev = m_s[...]
        m_new = jnp.maximum(m_prev, jnp.max(s, axis=-1, keepdims=True))
        alpha = jnp.exp(m_prev - m_new)
        pr = jnp.exp(s - m_new)
        l_s[...] = alpha * l_s[...] + jnp.sum(pr, axis=-1, keepdims=True)
        pb = pr.astype(BF16)
        pv = []
        for h in range(h_att):
            acc = None
            for g, vp in enumerate(v_pages):
                part = jnp.dot(pb[h * rows_h:(h + 1) * rows_h, g * PAGE_SIZE:(g + 1) * PAGE_SIZE], vp[h],
                               preferred_element_type=F32)
                acc = part if acc is None else acc + part
            pv.append(acc)
        acc_s[...] = alpha * acc_s[...] + jnp.concatenate(pv, axis=0)
        m_s[...] = m_new

    heads = lambda r: [r[h].astype(BF16) for h in range(h_att)]
    attend([heads(r) for r in kc_refs], [heads(r) for r in vc_refs], None)

    @pl.when(p == pl.num_programs(1) - 1)
    def _():
        shape = (h_att * rows_h, PAGE_SIZE)
        q_tok = lax.rem(lax.broadcasted_iota(jnp.int32, shape, 0), n_new)
        key = lax.broadcasted_iota(jnp.int32, shape, 1)
        head_cols = lambda ref: [ref[:, h * HEAD_DIM:(h + 1) * HEAD_DIM].astype(BF16) for h in range(h_att)]
        attend([head_cols(kn_s)], [head_cols(vn_s)], key <= q_tok)
        o_full = acc_s[...] / l_s[...]
        for h in range(h_att):
            o1 = o_full[h * rows_h:h * rows_h + n_new]
            o2 = o_full[h * rows_h + n_new:(h + 1) * rows_h]
            o = o1 - lam_ref[...] * o2
            o_ref[0, :, h * HEAD_DIM:(h + 1) * HEAD_DIM] = _sub_layer_norm(o, sg_ref[...], out_scale)


def _sample_attention(z3, cache_k, cache_v, layer, page_table, w_att, tabs, qg, kg, sg, lam_row, out_scale):
    bsz, n_new, _ = z3.shape
    n_pages = page_table.shape[1]
    h_att = w_att // HEAD_DIM
    g_cnt = _pick_tile(n_pages, (8, 4, 2, 1))
    cos, sa, sb = tabs
    z_spec = lambda c: pl.BlockSpec((1, n_new, w_att), functools.partial(lambda b, p, pt, c: (b, 0, c), c=c))
    cache_specs = [pl.BlockSpec((None, None, h_att, PAGE_SIZE, HEAD_DIM),
                                functools.partial(lambda b, p, pt, g: (layer, pt[b, p * g_cnt + g], 0, 0, 0), g=g))
                   for g in range(g_cnt)]
    tab_spec = pl.BlockSpec((n_new, HEAD_DIM), lambda b, p, pt: (0, 0))
    row_spec = pl.BlockSpec((1, HEAD_DIM), lambda b, p, pt: (0, 0))
    out_spec = pl.BlockSpec((1, n_new, w_att), lambda b, p, pt: (b, 0, 0))
    rows = h_att * 2 * n_new
    grid_spec = pltpu.PrefetchScalarGridSpec(
        num_scalar_prefetch=1,
        grid=(bsz, n_pages // g_cnt),
        in_specs=[z_spec(0), z_spec(1), z_spec(2)] + cache_specs + cache_specs
                 + [tab_spec, tab_spec, tab_spec, row_spec, row_spec, row_spec, row_spec],
        out_specs=(out_spec, out_spec, out_spec),
        scratch_shapes=[pltpu.VMEM((rows, HEAD_DIM), BF16),
                        pltpu.VMEM((PAGE_SIZE, w_att), F32), pltpu.VMEM((PAGE_SIZE, w_att), F32),
                        pltpu.VMEM((rows, 1), F32), pltpu.VMEM((rows, 1), F32),
                        pltpu.VMEM((rows, HEAD_DIM), F32)],
    )
    return pl.pallas_call(
        functools.partial(_sample_attn_kernel, h_att=h_att, n_new=n_new, pages_per_step=g_cnt,
                          out_scale=out_scale),
        out_shape=(jax.ShapeDtypeStruct((bsz, n_new, w_att), F32),) * 3,
        grid_spec=grid_spec,
        compiler_params=_params(("parallel", "arbitrary")),
        name="sample_paged_diff_attention",
    )(page_table, z3, z3, z3, *([cache_k] * g_cnt), *([cache_v] * g_cnt), cos, sa, sb, qg, kg, sg, lam_row)


def _retention_kernel(zq_ref, zk_ref, zv_ref, zg_ref, cos_ref, sin_ref, lg_ref, s0_ref,
                      o_ref, sfin_ref, *pad_s, chunk, n_chunks):
    cp = RET_CHUNK if chunk < RET_CHUNK else chunk
    lg = lg_ref[0]
    ri = lax.broadcasted_iota(jnp.int32, (cp, cp), 0)
    ci = lax.broadcasted_iota(jnp.int32, (cp, cp), 1)
    diff = (ri - ci).astype(F32)
    dec = jnp.where(diff >= 0, jnp.exp(jnp.maximum(diff, 0.0) * lg), 0.0)
    idx = lax.broadcasted_iota(jnp.int32, (cp, HEAD_DIM), 0).astype(F32)
    cross = jnp.exp((idx + 1.0) * lg)
    kdec_w = jnp.exp((chunk - 1.0 - idx) * lg)
    chunk_decay = jnp.exp(chunk * lg)

    def load(ref, r0):
        x = ref[pl.ds(r0, chunk), :]
        if cp == chunk:
            return x
        buf = pad_s[0]
        buf[...] = jnp.zeros(buf.shape, F32)
        buf[0:chunk, :] = x
        return buf[...]

    def body(c, state):
        r0 = pl.multiple_of(c * chunk, chunk)
        cos = load(cos_ref, r0)
        sin = load(sin_ref, r0)
        q = _ret_rope(load(zq_ref, r0), cos, sin)
        k = _ret_rope(load(zk_ref, r0), cos, sin) * (HEAD_DIM ** -0.5)
        v = load(zv_ref, r0)
        qb, kb, vb = q.astype(BF16), k.astype(BF16), v.astype(BF16)
        s = lax.dot_general(qb, kb, (((1,), (1,)), ((), ())), preferred_element_type=F32) * dec
        o = jnp.dot(s.astype(BF16), vb, preferred_element_type=F32)
        o = o + jnp.dot(qb, state.astype(BF16), preferred_element_type=F32) * cross
        kd = (k * kdec_w).astype(BF16)
        new_state = state * chunk_decay + lax.dot_general(kd, vb, (((0,), (0,)), ((), ())),
                                                          preferred_element_type=F32)
        ms = jnp.mean(o * o, axis=-1, keepdims=True)
        g = load(zg_ref, r0)
        out = o * lax.rsqrt(ms + EPS) * (g * _sigmoid(g))
        o_ref[pl.ds(r0, chunk), :] = out[0:chunk].astype(o_ref.dtype)
        return new_state

    sfin_ref[0, 0] = lax.fori_loop(0, n_chunks, body, s0_ref[0, 0], unroll=math.gcd(n_chunks, 4))


def _retention(z, b, l, col0, h_ret, tabs, log_gamma_rows, s0):
    m = z.shape[0]
    chunk = RET_CHUNK if l % RET_CHUNK == 0 else l
    n_chunks = l // chunk
    cos, sin = tabs
    zs = lambda sec: pl.BlockSpec((l, HEAD_DIM), functools.partial(
        lambda bb, h, sec: (bb, col0 + sec * h_ret + h), sec=sec))
    tab_spec = pl.BlockSpec((l, HEAD_DIM), lambda bb, h: (0, 0))
    scratch = [] if chunk == RET_CHUNK else [pltpu.VMEM((RET_CHUNK, HEAD_DIM), F32)]
    return pl.pallas_call(
        functools.partial(_retention_kernel, chunk=chunk, n_chunks=n_chunks),
        out_shape=(jax.ShapeDtypeStruct((m, h_ret * HEAD_DIM), F32 if chunk < 16 else BF16),
                   jax.ShapeDtypeStruct((b, h_ret, HEAD_DIM, HEAD_DIM), F32)),
        grid=(b, h_ret),
        in_specs=[zs(0), zs(1), zs(2), zs(3), tab_spec, tab_spec,
                  pl.BlockSpec((1, 1, HEAD_DIM), lambda bb, h: (h, 0, 0)),
                  pl.BlockSpec((1, 1, HEAD_DIM, HEAD_DIM), lambda bb, h: (bb, h, 0, 0))],
        out_specs=(pl.BlockSpec((l, HEAD_DIM), lambda bb, h: (bb, h)),
                   pl.BlockSpec((1, 1, HEAD_DIM, HEAD_DIM), lambda bb, h: (bb, h, 0, 0))),
        scratch_shapes=scratch,
        compiler_params=_params(("parallel", "parallel")),
        name="retention",
    )(z, z, z, z, cos, sin, log_gamma_rows, s0)


def _s5_kernel(u_ref, bb_ref, cc_ref, d_ref, are_ref, aim_ref, h0_ref, y_ref, xfin_ref, ui_s, bu_s, x_s,
               *, steps, n_seq):
    i = pl.program_id(1)
    nc = S5_STATE_COLS

    @pl.when(i == 0)
    def _():
        x_s[...] = h0_ref[0]

    if n_seq < SUBLANES:
        ui_s[...] = jnp.zeros(ui_s.shape, F32)
    for s in range(n_seq):
        ui_s[pl.ds(s, steps, stride=SUBLANES), :] = u_ref[s]
    u = ui_s[...]
    bu_s[...] = jnp.dot(u.astype(BF16), bb_ref[0], preferred_element_type=F32)
    a_re = jnp.broadcast_to(are_ref[0], (SUBLANES, nc))
    a_im = jnp.broadcast_to(aim_ref[0], (SUBLANES, nc))

    def body(t, carry):
        xr, xi = carry
        r0 = pl.multiple_of(t * SUBLANES, SUBLANES)
        nr = a_re * xr - a_im * xi + bu_s[pl.ds(r0, SUBLANES), 0:nc]
        ni = a_re * xi + a_im * xr + bu_s[pl.ds(r0, SUBLANES), nc:2 * nc]
        bu_s[pl.ds(r0, SUBLANES), 0:nc] = nr
        bu_s[pl.ds(r0, SUBLANES), nc:2 * nc] = ni
        return nr, ni

    xr, xi = lax.fori_loop(0, steps, body, (x_s[:, 0:nc], x_s[:, nc:2 * nc]), unroll=min(8, steps))
    x_s[:, 0:nc] = xr
    x_s[:, nc:2 * nc] = xi
    ui_s[...] = jnp.dot(bu_s[...].astype(BF16), cc_ref[0], preferred_element_type=F32) + d_ref[...] * u
    for s in range(n_seq):
        y_ref[s] = ui_s[pl.ds(s, steps, stride=SUBLANES), :]

    @pl.when(i == pl.num_programs(1) - 1)
    def _():
        xfin_ref[0] = x_s[...]


def _s5(z3, col_block0, w_ssm, bb, cc, d_row, a_re, a_im, h0):
    n_seq, n_steps, _ = z3.shape
    n_blocks = w_ssm // LANES
    t_chunk = _pick_tile(n_steps, (256, 128, 64, 32, 16, 8))
    r_chunk = t_chunk * SUBLANES
    return pl.pallas_call(
        functools.partial(_s5_kernel, steps=t_chunk, n_seq=n_seq),
        out_shape=(jax.ShapeDtypeStruct((n_seq, n_steps, w_ssm), F32),
                   jax.ShapeDtypeStruct((n_blocks, SUBLANES, 2 * S5_STATE_COLS), F32)),
        grid=(n_blocks, n_steps // t_chunk),
        in_specs=[pl.BlockSpec((n_seq, t_chunk, LANES), lambda j, i: (0, i, col_block0 + j)),
                  pl.BlockSpec((1, LANES, 2 * S5_STATE_COLS), lambda j, i: (j, 0, 0)),
                  pl.BlockSpec((1, 2 * S5_STATE_COLS, LANES), lambda j, i: (j, 0, 0)),
                  pl.BlockSpec((1, LANES), lambda j, i: (0, j)),
                  pl.BlockSpec((1, 1, S5_STATE_COLS), lambda j, i: (j, 0, 0)),
                  pl.BlockSpec((1, 1, S5_STATE_COLS), lambda j, i: (j, 0, 0)),
                  pl.BlockSpec((1, SUBLANES, 2 * S5_STATE_COLS), lambda j, i: (j, 0, 0))],
        out_specs=(pl.BlockSpec((n_seq, t_chunk, LANES), lambda j, i: (0, i, j)),
                   pl.BlockSpec((1, SUBLANES, 2 * S5_STATE_COLS), lambda j, i: (j, 0, 0))),
        scratch_shapes=[pltpu.VMEM((r_chunk, LANES), F32),
                        pltpu.VMEM((r_chunk, 2 * S5_STATE_COLS), F32),
                        pltpu.VMEM((SUBLANES, 2 * S5_STATE_COLS), F32)],
        compiler_params=_params(("parallel", "arbitrary")),
        name="s5_scan",
    )(z3, bb, cc, d_row, a_re, a_im, h0)


def _s5_post_kernel(y_ref, w_ref, g_ref, o_ref):
    y = y_ref[...]
    y = 0.5 * y * (1.0 + jnp.tanh(math.sqrt(2.0 / math.pi) * (y + 0.044715 * (y * y * y))))
    gate = _sigmoid(jnp.dot(y.astype(BF16), w_ref[...], preferred_element_type=F32))
    y = y * gate
    ms = jnp.mean(y * y, axis=-1, keepdims=True)
    o_ref[...] = (y * lax.rsqrt(ms + EPS) * g_ref[...]).astype(o_ref.dtype)


def _s5_post(y, w_glu, g):
    m, w = y.shape
    tm = _pick_tile(m, (512, 256, 128, 64))
    return pl.pallas_call(
        _s5_post_kernel,
        out_shape=jax.ShapeDtypeStruct((m, w), BF16),
        grid=(m // tm,),
        in_specs=[pl.BlockSpec((tm, w), lambda i: (i, 0)), pl.BlockSpec((w, w), lambda i: (0, 0)),
                  pl.BlockSpec((1, w), lambda i: (0, 0))],
        out_specs=pl.BlockSpec((tm, w), lambda i: (i, 0)),
        compiler_params=_params(("parallel",)),
        name="s5_gelu_glu_norm",
    )(y, w_glu, g.reshape(1, w))


def _s5_discretise(lam_re, lam_im, log_dt, b_re, b_im, c_re, c_im):
    g_cnt = lam_re.shape[0]
    dt = jnp.exp(log_dt.astype(F32))[:, None]
    mag = jnp.exp(lam_re * dt)
    ang = lam_im * dt
    a_re, a_im = mag * jnp.cos(ang), mag * jnp.sin(ang)
    den = lam_re * lam_re + lam_im * lam_im
    num_re = a_re - 1.0
    coef_re = (num_re * lam_re + a_im * lam_im) / den
    coef_im = (a_im * lam_re - num_re * lam_im) / den
    bb_re = coef_re[..., None] * b_re - coef_im[..., None] * b_im
    bb_im = coef_re[..., None] * b_im + coef_im[..., None] * b_re
    nb = g_cnt // S5_GROUPS_PER_BLOCK
    gb = S5_GROUPS_PER_BLOCK
    eye = jnp.eye(gb, dtype=F32)

    def block_diag_in(w):
        w = w.reshape(nb, gb, SSM_P, SSM_H)
        return jnp.einsum('ngph,gk->nghkp', w, eye).reshape(nb, gb * SSM_H, gb * SSM_P)

    def block_diag_out(w):
        w = w.reshape(nb, gb, SSM_H, SSM_P)
        return jnp.einsum('nghp,gk->ngpkh', w, eye).reshape(nb, gb * SSM_P, gb * SSM_H)

    bb = jnp.concatenate([block_diag_in(bb_re), block_diag_in(bb_im)], axis=2).astype(BF16)
    cc = jnp.concatenate([block_diag_out(c_re.astype(F32)), block_diag_out(-c_im.astype(F32))], axis=1).astype(BF16)
    a_re_b = a_re.reshape(nb, 1, gb * SSM_P)
    a_im_b = a_im.reshape(nb, 1, gb * SSM_P)
    return bb, cc, a_re_b, a_im_b


def _s5_mixer(z, b, l, col0, w_ssm, prm, h0_re, h0_im):
    nb = w_ssm // LANES
    gb = S5_GROUPS_PER_BLOCK
    def pack_state(h):
        h = jnp.pad(h.astype(F32), ((0, SUBLANES - b), (0, 0), (0, 0)))
        return h.reshape(SUBLANES, nb, gb * SSM_P).transpose(1, 0, 2)

    h0 = jnp.concatenate([pack_state(h0_re), pack_state(h0_im)], axis=2)
    y, xfin = _s5(z.reshape(b, l, z.shape[1]), col0 // LANES, w_ssm,
                  prm['bb'], prm['cc'], prm['d'], prm['a_re'], prm['a_im'], h0)
    o = _s5_post(y.reshape(b * l, w_ssm), prm['w_glu'], prm['norm_g'])

    def unpack_state(x):
        return x.transpose(1, 0, 2).reshape(SUBLANES, nb * gb, SSM_P)[:b]

    return o, unpack_state(xfin[:, :, :S5_STATE_COLS]), unpack_state(xfin[:, :, S5_STATE_COLS:])


def _att_rope_tables(pos):
    freqs = ROPE_THETA ** (-jnp.arange(0, DK_HALF, 2, dtype=F32) / DK_HALF)
    ang = pos.astype(F32)[:, None] * freqs[None, :]
    c, s = jnp.cos(ang), jnp.sin(ang)
    zero = jnp.zeros_like(s)
    return (jnp.concatenate([c, c, c, c], axis=-1),
            jnp.concatenate([-s, zero, -s, zero], axis=-1),
            jnp.concatenate([zero, s, zero, s], axis=-1))


def _ret_rope_tables(pos):
    freqs = 1.0 / (ROPE_THETA ** jnp.linspace(0.0, 1.0, HEAD_DIM // 2, dtype=F32))
    ang = pos.astype(F32)[:, None] * freqs[None, :]
    c, s = jnp.cos(ang), jnp.sin(ang)
    return jnp.concatenate([c, c], axis=-1), jnp.concatenate([-s, s], axis=-1)


def _row128(v):
    return jnp.tile(v.astype(F32), HEAD_DIM // v.shape[0]).reshape(1, HEAD_DIM)


def _mixers(z, b, l, lw, attend, ret_tabs, ret_s0, s5_h0_re, s5_h0_im):
    d = lw['d_model']
    w_att = lw['w_att']
    o_att, new_k, new_v = attend(z)
    o_ret, s_ret = _retention(z, b, l, 3 * w_att // HEAD_DIM, w_att // HEAD_DIM, ret_tabs, lw['log_gamma'], ret_s0)
    o_ssm, s5_re, s5_im = _s5_mixer(z, b, l, 7 * w_att, d - 2 * w_att, lw['s5'], s5_h0_re, s5_h0_im)
    return [o_att, o_ret.astype(BF16), o_ssm], (new_k, new_v, s_ret, s5_re, s5_im)


def kernel(x_prompt, x_sample, cache_k, cache_v, page_table, state_ret, state_s5_re, state_s5_im,
           norm_mix_g, w_in, q_norm_g, k_norm_g, lambda_q1, lambda_k1, lambda_q2, lambda_k2, subln_g,
           s5_lambda_re, s5_lambda_im, s5_log_dt, s5_b_re, s5_b_im, s5_c_re, s5_c_im, s5_d, s5_w_glu,
           s5_norm_g, w_out, norm_ffn_g, w_gate, w_up, w_down):
    b_p, seq, d = x_prompt.shape
    b_s, n_new, _ = x_sample.shape
    depth = w_in.shape[0]
    w_att = (3 * d) // (8 * HEAD_DIM) * HEAD_DIM
    h_att = w_att // HEAD_DIM
    w_ssm = d - 2 * w_att
    n_pages = page_table.shape[1]
    past = n_pages * PAGE_SIZE
    assert (2 * w_att) % w_ssm == 0 and w_ssm % LANES == 0 and b_p <= SUBLANES and b_s <= SUBLANES
    ssm_rb = (2 * w_att) // w_ssm

    pos_p = jnp.arange(seq, dtype=jnp.int32)
    pos_s = past + jnp.arange(n_new, dtype=jnp.int32)
    att_tabs_p, att_tabs_s = _att_rope_tables(pos_p), _att_rope_tables(pos_s)
    ret_tabs_p, ret_tabs_s = _ret_rope_tables(pos_p), _ret_rope_tables(pos_s)
    log_gamma = jnp.log(1.0 - 2.0 ** (-5.0 - jnp.arange(h_att, dtype=F32)))
    log_gamma_rows = jnp.broadcast_to(log_gamma[:, None, None], (h_att, 1, HEAD_DIM))

    xp = x_prompt.reshape(b_p * seq, d)
    xs = x_sample.reshape(b_s * n_new, d)
    zeros_ret = jnp.zeros((b_p, h_att, HEAD_DIM, HEAD_DIM), F32)
    zeros_s5 = jnp.zeros((b_p, w_ssm // SSM_H, SSM_P), F32)
    w_down_b = w_down.astype(BF16)
    cache_kh = jnp.transpose(cache_k, (0, 1, 3, 2, 4))
    cache_vh = jnp.transpose(cache_v, (0, 1, 3, 2, 4))
    outs = [[] for _ in range(10)]
    for layer in range(depth):
        lam_init = 0.8 - 0.6 * math.exp(-0.3 * layer)
        lam = (jnp.exp(jnp.sum(lambda_q1[layer].astype(F32) * lambda_k1[layer].astype(F32)))
               - jnp.exp(jnp.sum(lambda_q2[layer].astype(F32) * lambda_k2[layer].astype(F32))) + lam_init)
        lam_row = jnp.broadcast_to(lam.reshape(1, 1), (1, HEAD_DIM)).astype(F32)
        bb, cc, a_re, a_im = _s5_discretise(
            s5_lambda_re[layer].astype(F32), s5_lambda_im[layer].astype(F32), s5_log_dt[layer],
            s5_b_re[layer].astype(F32), s5_b_im[layer].astype(F32), s5_c_re[layer], s5_c_im[layer])
        lw = {
            'd_model': d, 'w_att': w_att, 'log_gamma': log_gamma_rows,
            's5': {'bb': bb, 'cc': cc, 'a_re': a_re, 'a_im': a_im,
                   'd': s5_d[layer].astype(F32).reshape(1, w_ssm),
                   'w_glu': s5_w_glu[layer].astype(BF16), 'norm_g': s5_norm_g[layer].astype(F32)},
        }
        qg, kg, sg = _row128(q_norm_g[layer]), _row128(k_norm_g[layer]), _row128(subln_g[layer])
        out_scale = 1.0 - lam_init

        hp, hs = _rmsnorm(xp, norm_mix_g[layer]), _rmsnorm(xs, norm_mix_g[layer])
        zp, zs = _matmul_pair([hp], [hs], w_in, layer, [0], tm=1024, tn=512, name="in_proj")

        attend_p = functools.partial(_prompt_attention, b=b_p, l=seq, w_att=w_att, tabs=att_tabs_p,
                                     qg=qg, kg=kg, sg=sg, lam_row=lam_row, out_scale=out_scale)
        mix_p, (kp, vp, rp, p_re, p_im) = _mixers(zp, b_p, seq, lw, attend_p, ret_tabs_p,
                                                  zeros_ret, zeros_s5, zeros_s5)

        def attend_s(z):
            o, nk, nv = _sample_attention(z.reshape(b_s, n_new, z.shape[1]), cache_kh, cache_vh, layer,
                                          page_table, w_att, att_tabs_s, qg, kg, sg, lam_row, out_scale)
            flat = lambda t: t.reshape(b_s * n_new, w_att)
            return flat(o).astype(BF16), flat(nk), flat(nv)

        mix_s, (k_s, v_s, r_s, s_re, s_im) = _mixers(zs, b_s, n_new, lw, attend_s, ret_tabs_s, state_ret[layer],
                                                     state_s5_re[layer], state_s5_im[layer])

        xp, xs = _matmul_pair(mix_p, mix_s, w_out, layer, [0, 1, ssm_rb], mode="res", res_p=xp, res_s=xs,
                              tm=1024, tn=512, name="out_proj")
        hp, hs = _rmsnorm(xp, norm_ffn_g[layer]), _rmsnorm(xs, norm_ffn_g[layer])
        act_p, act_s = _matmul_pair([hp], [hs], w_gate, layer, [0], mode="swiglu", w2=w_up,
                                    out_dtype=BF16, tm=1024, tn=256, name="ffn_gate_up")
        xp = _matmul_res(act_p, w_down_b, layer, xp, tm=512, tn=512, name="ffn_down")
        xs = _matmul_res(act_s, w_down_b, layer, xs, tm=512, tn=512, name="ffn_down")

        for lst, val in zip(outs, (kp, vp,
                                   k_s.reshape(b_s, n_new, h_att, HEAD_DIM), v_s.reshape(b_s, n_new, h_att, HEAD_DIM),
                                   rp, r_s, p_re, p_im, s_re, s_im)):
            lst.append(val)
    stacked = [jnp.stack(o) for o in outs]
    for idx in (0, 1):
        stacked[idx] = jnp.transpose(stacked[idx], (0, 1, 3, 2, 4))
    return (xp.reshape(b_p, seq, d), xs.reshape(b_s, n_new, d)) + tuple(stacked)
```

```python
import functools
import math

import jax
import jax.numpy as jnp
from jax import lax
from jax.experimental import pallas as pl
from jax.experimental.pallas import tpu as pltpu

F32 = jnp.float32
BF16 = jnp.bfloat16

HEAD_DIM = 128
DK_HALF = HEAD_DIM // 2
SSM_H = 16
SSM_P = 64
PAGE_SIZE = 128
RET_CHUNK = 128
ROPE_THETA = 10000.0
EPS = 1e-6
LANES = 128
SUBLANES = 8
S5_GROUPS_PER_BLOCK = LANES // SSM_H
S5_STATE_COLS = S5_GROUPS_PER_BLOCK * SSM_P
NEG_BIG = -1e30
V7X_VMEM_LIMIT_BYTES = 56 * 1024 * 1024


def _pick_tile(n, candidates):
    for c in candidates:
        if n % c == 0:
            return c
    return n


def _params(semantics):
    return pltpu.CompilerParams(dimension_semantics=semantics, vmem_limit_bytes=V7X_VMEM_LIMIT_BYTES)


def _sigmoid(x):
    return 1.0 / (1.0 + jnp.exp(-x))


def _rmsnorm_kernel(x_ref, g_ref, o_ref):
    x = x_ref[...]
    ms = jnp.mean(x * x, axis=-1, keepdims=True)
    o_ref[...] = (x * lax.rsqrt(ms + EPS) * g_ref[...]).astype(o_ref.dtype)


def _rmsnorm(x, g):
    m, d = x.shape
    tm = _pick_tile(m, (512, 256, 128, 64))
    return pl.pallas_call(
        _rmsnorm_kernel,
        out_shape=jax.ShapeDtypeStruct((m, d), BF16),
        grid=(m // tm,),
        in_specs=[pl.BlockSpec((tm, d), lambda i: (i, 0)), pl.BlockSpec((1, d), lambda i: (0, 0))],
        out_specs=pl.BlockSpec((tm, d), lambda i: (i, 0)),
        compiler_params=_params(("parallel",)),
        name="rmsnorm",
    )(x, g.reshape(1, d))


def _mm_res_kernel(a_ref, b_ref, res_ref, o_ref):
    o_ref[...] = res_ref[...] + jnp.dot(a_ref[...], b_ref[...], preferred_element_type=F32)


def _matmul_res(a, w_all, layer, res, *, tm, tn, name):
    m, k = a.shape
    n = w_all.shape[2]
    tm = _pick_tile(m, (tm, 512, 256, 128, 64))
    tn = _pick_tile(n, (tn, 512, 256, 128))
    return pl.pallas_call(
        _mm_res_kernel,
        out_shape=jax.ShapeDtypeStruct((m, n), F32),
        grid=(m // tm, n // tn),
        in_specs=[pl.BlockSpec((tm, k), lambda i, j: (i, 0)),
                  pl.BlockSpec((None, k, tn), lambda i, j: (layer, 0, j)),
                  pl.BlockSpec((tm, tn), lambda i, j: (i, j))],
        out_specs=pl.BlockSpec((tm, tn), lambda i, j: (i, j)),
        compiler_params=_params(("parallel", "arbitrary")),
        name=name,
    )(a, w_all, res)


def _mm_pair_kernel(*refs, n_a, n_w, mode):
    ap_refs, as_refs = refs[:n_a], refs[n_a:2 * n_a]
    w_refs = refs[2 * n_a:2 * n_a + n_w]
    pos = 2 * n_a + n_w
    res_refs = refs[pos:pos + 2] if mode == "res" else ()
    pos += len(res_refs)
    op_ref, os_ref = refs[pos], refs[pos + 1]
    wb_refs = refs[pos + 2:]
    first = pl.program_id(1) == 0

    @pl.when(first)
    def _():
        for w_ref, wb_ref in zip(w_refs, wb_refs):
            wb_ref[...] = w_ref[...].astype(BF16)

    def product(a_vals):
        if mode == "swiglu":
            g = jnp.dot(a_vals[0], wb_refs[0][...], preferred_element_type=F32)
            u = jnp.dot(a_vals[0], wb_refs[1][...], preferred_element_type=F32)
            return g * _sigmoid(g) * u
        acc = jnp.dot(a_vals[0], wb_refs[0][...], preferred_element_type=F32)
        for a, wb_ref in zip(a_vals[1:], wb_refs[1:]):
            acc = acc + jnp.dot(a, wb_ref[...], preferred_element_type=F32)
        return acc

    def store(o_ref, val, res_ref):
        o_ref[...] = (val if res_ref is None else res_ref[...] + val).astype(o_ref.dtype)

    res_p, res_s = res_refs if res_refs else (None, None)
    tm = op_ref.shape[0]

    @pl.when(first)
    def _():
        both = product([jnp.concatenate([ap[...], as_[...]], axis=0) for ap, as_ in zip(ap_refs, as_refs)])
        store(op_ref, both[:tm], res_p)
        store(os_ref, both[tm:], res_s)

    @pl.when(jnp.logical_not(first))
    def _():
        store(op_ref, product([ap[...] for ap in ap_refs]), res_p)


def _matmul_pair(ap_list, as_list, w, layer, row_blocks, *, mode="plain", res_p=None, res_s=None, w2=None,
                 out_dtype=F32, tm, tn, name):
    mp, ms = ap_list[0].shape[0], as_list[0].shape[0]
    n = w.shape[2]
    tm = _pick_tile(mp, (tm, 512, 256, 128, 64))
    tn = _pick_tile(n, (tn, 512, 256, 128))
    in_specs = [pl.BlockSpec((tm, a.shape[1]), lambda j, i: (i, 0)) for a in ap_list]
    in_specs += [pl.BlockSpec((ms, a.shape[1]), lambda j, i: (0, 0)) for a in as_list]
    args = list(ap_list) + list(as_list)
    scratch = []
    weights = [(w, a.shape[1], rb) for a, rb in zip(ap_list, row_blocks)]
    if mode == "swiglu":
        weights.append((w2, ap_list[0].shape[1], 0))
    for wt, rows, rb in weights:
        in_specs.append(pl.BlockSpec((None, rows, tn), functools.partial(lambda j, i, rb: (layer, rb, j), rb=rb)))
        args.append(wt)
        scratch.append(pltpu.VMEM((rows, tn), BF16))
    if mode == "res":
        in_specs += [pl.BlockSpec((tm, tn), lambda j, i: (i, j)), pl.BlockSpec((ms, tn), lambda j, i: (0, j))]
        args += [res_p, res_s]
    return pl.pallas_call(
        functools.partial(_mm_pair_kernel, n_a=len(ap_list), n_w=len(weights), mode=mode),
        out_shape=(jax.ShapeDtypeStruct((mp, n), out_dtype), jax.ShapeDtypeStruct((ms, n), out_dtype)),
        grid=(n // tn, mp // tm),
        in_specs=in_specs,
        out_specs=(pl.BlockSpec((tm, tn), lambda j, i: (i, j)), pl.BlockSpec((ms, tn), lambda j, i: (0, j))),
        scratch_shapes=scratch,
        compiler_params=_params(("parallel", "arbitrary")),
        name=name,
    )(*args)


def _half_rmsnorm(x, g):
    lane = lax.broadcasted_iota(jnp.int32, x.shape, 1)
    first = lane < DK_HALF
    sq = x * x
    tot = jnp.sum(sq, axis=-1, keepdims=True)
    s1 = jnp.sum(jnp.where(first, sq, 0.0), axis=-1, keepdims=True)
    ms = jnp.where(first, s1, tot - s1) * (1.0 / DK_HALF)
    return x * lax.rsqrt(ms + EPS) * g


def _att_rope(x, cos, sin_a, sin_b):
    return x * cos + pltpu.roll(x, LANES - DK_HALF // 2, 1) * sin_a + pltpu.roll(x, DK_HALF // 2, 1) * sin_b


def _ret_rope(x, cos, sin_signed):
    return x * cos + pltpu.roll(x, DK_HALF, 1) * sin_signed


def _stack_half_masked(q):
    lane = lax.broadcasted_iota(jnp.int32, q.shape, 1)
    first = lane < DK_HALF
    return jnp.concatenate([jnp.where(first, q, 0.0), jnp.where(first, 0.0, q)], axis=0)


def _sub_layer_norm(o, g, out_scale):
    ms = jnp.mean(o * o, axis=-1, keepdims=True)
    return o * lax.rsqrt(ms + EPS) * g * out_scale


def _prompt_attn_kernel(zq_ref, zk_ref, zv_ref, cq_ref, saq_ref, sbq_ref, ck_ref, sak_ref, sbk_ref,
                        qg_ref, kg_ref, sg_ref, lam_ref,
                        o_ref, newk_ref, newv_ref,
                        k_s, vt_s, q2_s, m_s, l_s, acc_s, *, tq, n_tiles, n_chains, out_scale):
    i = pl.program_id(2)

    @pl.when(i == 0)
    def _():
        k = _att_rope(_half_rmsnorm(zk_ref[...], kg_ref[...]), ck_ref[...], sak_ref[...], sbk_ref[...])
        newk_ref[...] = k
        k_s[...] = k.astype(BF16)
        v = zv_ref[...]
        newv_ref[...] = v
        for t in range(n_tiles):
            vt_s[t] = v[t * tq:(t + 1) * tq].T.astype(BF16)

    q = _att_rope(_half_rmsnorm(zq_ref[...], qg_ref[...]), cq_ref[...], saq_ref[...], sbq_ref[...])
    q2_s[...] = _stack_half_masked(q * (DK_HALF ** -0.5)).astype(BF16)
    m_s[...] = jnp.full(m_s.shape, NEG_BIG, F32)
    l_s[...] = jnp.zeros(l_s.shape, F32)
    acc_s[...] = jnp.zeros(acc_s.shape, F32)

    wc = (2 * tq) // n_chains

    def block(j, masked):
        kj = k_s[pl.ds(pl.multiple_of(j * tq, tq), tq), :]
        vj = vt_s[j]
        for c in range(n_chains):
            cols = slice(c * wc, (c + 1) * wc)
            s = lax.dot_general(kj, q2_s[cols, :], (((1,), (1,)), ((), ())), preferred_element_type=F32)
            if masked:
                key = lax.broadcasted_iota(jnp.int32, s.shape, 0)
                qry = lax.broadcasted_iota(jnp.int32, s.shape, 1) + c * wc
                qry = jnp.where(qry >= tq, qry - tq, qry)
                s = jnp.where(key <= qry, s, NEG_BIG)
            m_prev = m_s[:, cols]
            m_new = jnp.maximum(m_prev, jnp.max(s, axis=0, keepdims=True))
            alpha = jnp.exp(m_prev - m_new)
            p = jnp.exp(s - m_new)
            l_s[:, cols] = alpha * l_s[:, cols] + jnp.sum(p, axis=0, keepdims=True)
            acc_s[:, cols] = alpha * acc_s[:, cols] + jnp.dot(vj, p.astype(BF16), preferred_element_type=F32)
            m_s[:, cols] = m_new

    def body(j, carry):
        block(j, False)
        return carry

    lax.fori_loop(0, i, body, 0)
    block(i, True)

    o_full = acc_s[...] / l_s[...]
    o_t = o_full[:, :tq] - lam_ref[0:1, 0:1] * o_full[:, tq:]
    o_ref[...] = _sub_layer_norm(o_t.T, sg_ref[...], out_scale).astype(o_ref.dtype)


def _prompt_attn_kernel_aliased(*refs, n_in, **kw):
    _prompt_attn_kernel(*refs[:n_in], *refs[n_in + 2:], **kw)


def _prompt_attention(z, b, l, w_att, tabs, qg, kg, sg, lam_row, out_scale, layer, depth, kv_prev):
    m = z.shape[0]
    h_att = w_att // HEAD_DIM
    tq = _pick_tile(l, (512, 256, 128))
    nq = l // tq
    cos, sa, sb = tabs
    q_spec = pl.BlockSpec((tq, HEAD_DIM), lambda bb, h, i: (bb * nq + i, h))
    k_spec = pl.BlockSpec((l, HEAD_DIM), lambda bb, h, i: (bb, h_att + h))
    v_spec = pl.BlockSpec((l, HEAD_DIM), lambda bb, h, i: (bb, 2 * h_att + h))
    tq_spec = pl.BlockSpec((tq, HEAD_DIM), lambda bb, h, i: (i, 0))
    tk_spec = pl.BlockSpec((l, HEAD_DIM), lambda bb, h, i: (0, 0))
    row_spec = pl.BlockSpec((1, HEAD_DIM), lambda bb, h, i: (0, 0))
    kv_out_spec = pl.BlockSpec((None, None, None, l, HEAD_DIM), lambda bb, h, i: (layer, bb, h, 0, 0))
    in_specs = [q_spec, k_spec, v_spec, tq_spec, tq_spec, tq_spec, tk_spec, tk_spec, tk_spec,
                row_spec, row_spec, row_spec, row_spec]
    args = [z, z, z, cos, sa, sb, cos, sa, sb, qg, kg, sg, lam_row]
    kw = dict(tq=tq, n_tiles=nq, n_chains=1, out_scale=out_scale)
    if kv_prev is None:
        body, aliases = functools.partial(_prompt_attn_kernel, **kw), {}
    else:
        body = functools.partial(_prompt_attn_kernel_aliased, n_in=len(args), **kw)
        aliases = {len(args): 1, len(args) + 1: 2}
        in_specs = in_specs + [pl.BlockSpec(memory_space=pl.ANY)] * 2
        args = args + list(kv_prev)
    kv_shape = jax.ShapeDtypeStruct((depth, b, h_att, l, HEAD_DIM), F32)
    return pl.pallas_call(
        body,
        out_shape=(jax.ShapeDtypeStruct((m, w_att), BF16), kv_shape, kv_shape),
        grid=(b, h_att, nq),
        in_specs=in_specs,
        out_specs=(q_spec, kv_out_spec, kv_out_spec),
        input_output_aliases=aliases,
        scratch_shapes=[pltpu.VMEM((l, HEAD_DIM), BF16), pltpu.VMEM((nq, HEAD_DIM, tq), BF16),
                        pltpu.VMEM((2 * tq, HEAD_DIM), BF16),
                        pltpu.VMEM((1, 2 * tq), F32), pltpu.VMEM((1, 2 * tq), F32),
                        pltpu.VMEM((HEAD_DIM, 2 * tq), F32)],
        compiler_params=_params(("parallel", "parallel", "arbitrary")),
        name="prompt_diff_attention",
    )(*args)


def _sample_attn_kernel(*refs, h_att, n_new, pages_per_step, out_scale):
    g_cnt = pages_per_step
    zq_ref, zk_ref, zv_ref = refs[1:4]
    kc_refs = refs[4:4 + g_cnt]
    vc_refs = refs[4 + g_cnt:4 + 2 * g_cnt]
    cos_ref, sa_ref, sb_ref, qg_ref, kg_ref, sg_ref, lam_ref = refs[4 + 2 * g_cnt:11 + 2 * g_cnt]
    o_ref, newk_ref, newv_ref, q2_s, kn_s, vn_s, m_s, l_s, acc_s = refs[11 + 2 * g_cnt:]
    p = pl.program_id(1)
    rows_h = 2 * n_new

    @pl.when(p == 0)
    def _():
        kn_s[...] = jnp.zeros(kn_s.shape, F32)
        vn_s[...] = jnp.zeros(vn_s.shape, F32)
        for h in range(h_att):
            cs = slice(h * HEAD_DIM, (h + 1) * HEAD_DIM)
            q = _att_rope(_half_rmsnorm(zq_ref[0, :, cs], qg_ref[...]), cos_ref[...], sa_ref[...], sb_ref[...])
            q2_s[h * rows_h:(h + 1) * rows_h, :] = _stack_half_masked(q * (DK_HALF ** -0.5)).astype(BF16)
            k = _att_rope(_half_rmsnorm(zk_ref[0, :, cs], kg_ref[...]), cos_ref[...], sa_ref[...], sb_ref[...])
            newk_ref[0, :, cs] = k
            kn_s[0:n_new, cs] = k
        v = zv_ref[0]
        newv_ref[0] = v
        vn_s[0:n_new, :] = v
        m_s[...] = jnp.full(m_s.shape, NEG_BIG, F32)
        l_s[...] = jnp.zeros(l_s.shape, F32)
        acc_s[...] = jnp.zeros(acc_s.shape, F32)

    def attend(k_pages, v_pages, mask):
        s = jnp.concatenate(
            [jnp.concatenate(
                [lax.dot_general(q2_s[h * rows_h:(h + 1) * rows_h, :], kp[h], (((1,), (1,)), ((), ())),
                                 preferred_element_type=F32) for kp in k_pages], axis=1)
             for h in range(h_att)], axis=0)
        if mask is not None:
            s = jnp.where(mask, s, NEG_BIG)
        m_prev = m_s[...]
        m_new = jnp.maximum(m_prev, jnp.max(s, axis=-1, keepdims=True))
        alpha = jnp.exp(m_prev - m_new)
        pr = jnp.exp(s - m_new)
        l_s[...] = alpha * l_s[...] + jnp.sum(pr, axis=-1, keepdims=True)
        pb = pr.astype(BF16)
        pv = []
        for h in range(h_att):
            acc = None
            for g, vp in enumerate(v_pages):
                part = jnp.dot(pb[h * rows_h:(h + 1) * rows_h, g * PAGE_SIZE:(g + 1) * PAGE_SIZE], vp[h],
                               preferred_element_type=F32)
                acc = part if acc is None else acc + part
            pv.append(acc)
        acc_s[...] = alpha * acc_s[...] + jnp.concatenate(pv, axis=0)
        m_s[...] = m_new

    heads = lambda r: [r[h].astype(BF16) for h in range(h_att)]
    attend([heads(r) for r in kc_refs], [heads(r) for r in vc_refs], None)

    @pl.when(p == pl.num_programs(1) - 1)
    def _():
        shape = (h_att * rows_h, PAGE_SIZE)
        q_tok = lax.rem(lax.broadcasted_iota(jnp.int32, shape, 0), n_new)
        key = lax.broadcasted_iota(jnp.int32, shape, 1)
        head_cols = lambda ref: [ref[:, h * HEAD_DIM:(h + 1) * HEAD_DIM].astype(BF16) for h in range(h_att)]
        attend([head_cols(kn_s)], [head_cols(vn_s)], key <= q_tok)
        o_full = acc_s[...] / l_s[...]
        for h in range(h_att):
            o1 = o_full[h * rows_h:h * rows_h + n_new]
            o2 = o_full[h * rows_h + n_new:(h + 1) * rows_h]
            o = o1 - lam_ref[...] * o2
            o_ref[0, :, h * HEAD_DIM:(h + 1) * HEAD_DIM] = _sub_layer_norm(o, sg_ref[...], out_scale)


def _sample_attention(z3, cache_k, cache_v, layer, page_table, w_att, tabs, qg, kg, sg, lam_row, out_scale):
    bsz, n_new, _ = z3.shape
    n_pages = page_table.shape[1]
    h_att = w_att // HEAD_DIM
    g_cnt = _pick_tile(n_pages, (8, 4, 2, 1))
    cos, sa, sb = tabs
    z_spec = lambda c: pl.BlockSpec((1, n_new, w_att), functools.partial(lambda b, p, pt, c: (b, 0, c), c=c))
    cache_specs = [pl.BlockSpec((None, None, h_att, PAGE_SIZE, HEAD_DIM),
                                functools.partial(lambda b, p, pt, g: (layer, pt[b, p * g_cnt + g], 0, 0, 0), g=g))
                   for g in range(g_cnt)]
    tab_spec = pl.BlockSpec((n_new, HEAD_DIM), lambda b, p, pt: (0, 0))
    row_spec = pl.BlockSpec((1, HEAD_DIM), lambda b, p, pt: (0, 0))
    out_spec = pl.BlockSpec((1, n_new, w_att), lambda b, p, pt: (b, 0, 0))
    rows = h_att * 2 * n_new
    grid_spec = pltpu.PrefetchScalarGridSpec(
        num_scalar_prefetch=1,
        grid=(bsz, n_pages // g_cnt),
        in_specs=[z_spec(0), z_spec(1), z_spec(2)] + cache_specs + cache_specs
                 + [tab_spec, tab_spec, tab_spec, row_spec, row_spec, row_spec, row_spec],
        out_specs=(out_spec, out_spec, out_spec),
        scratch_shapes=[pltpu.VMEM((rows, HEAD_DIM), BF16),
                        pltpu.VMEM((PAGE_SIZE, w_att), F32), pltpu.VMEM((PAGE_SIZE, w_att), F32),
                        pltpu.VMEM((rows, 1), F32), pltpu.VMEM((rows, 1), F32),
                        pltpu.VMEM((rows, HEAD_DIM), F32)],
    )
    return pl.pallas_call(
        functools.partial(_sample_attn_kernel, h_att=h_att, n_new=n_new, pages_per_step=g_cnt,
                          out_scale=out_scale),
        out_shape=(jax.ShapeDtypeStruct((bsz, n_new, w_att), F32),) * 3,
        grid_spec=grid_spec,
        compiler_params=_params(("parallel", "arbitrary")),
        name="sample_paged_diff_attention",
    )(page_table, z3, z3, z3, *([cache_k] * g_cnt), *([cache_v] * g_cnt), cos, sa, sb, qg, kg, sg, lam_row)


def _retention_kernel(zq_ref, zk_ref, zv_ref, zg_ref, cos_ref, sin_ref, lg_ref, s0_ref,
                      o_ref, sfin_ref, *pad_s, chunk, n_chunks):
    cp = RET_CHUNK if chunk < RET_CHUNK else chunk
    lg = lg_ref[0]
    ri = lax.broadcasted_iota(jnp.int32, (cp, cp), 0)
    ci = lax.broadcasted_iota(jnp.int32, (cp, cp), 1)
    diff = (ri - ci).astype(F32)
    dec = jnp.where(diff >= 0, jnp.exp(jnp.maximum(diff, 0.0) * lg), 0.0)
    idx = lax.broadcasted_iota(jnp.int32, (cp, HEAD_DIM), 0).astype(F32)
    cross = jnp.exp((idx + 1.0) * lg)
    kdec_w = jnp.exp((chunk - 1.0 - idx) * lg)
    chunk_decay = jnp.exp(chunk * lg)

    def load(ref, r0):
        x = ref[pl.ds(r0, chunk), :]
        if cp == chunk:
            return x
        buf = pad_s[0]
        buf[...] = jnp.zeros(buf.shape, F32)
        buf[0:chunk, :] = x
        return buf[...]

    def body(c, state):
        r0 = pl.multiple_of(c * chunk, chunk)
        cos = load(cos_ref, r0)
        sin = load(sin_ref, r0)
        q = _ret_rope(load(zq_ref, r0), cos, sin)
        k = _ret_rope(load(zk_ref, r0), cos, sin) * (HEAD_DIM ** -0.5)
        v = load(zv_ref, r0)
        qb, kb, vb = q.astype(BF16), k.astype(BF16), v.astype(BF16)
        s = lax.dot_general(qb, kb, (((1,), (1,)), ((), ())), preferred_element_type=F32) * dec
        o = jnp.dot(s.astype(BF16), vb, preferred_element_type=F32)
        o = o + jnp.dot(qb, state.astype(BF16), preferred_element_type=F32) * cross
        kd = (k * kdec_w).astype(BF16)
        new_state = state * chunk_decay + lax.dot_general(kd, vb, (((0,), (0,)), ((), ())),
                                                          preferred_element_type=F32)
        ms = jnp.mean(o * o, axis=-1, keepdims=True)
        g = load(zg_ref, r0)
        out = o * lax.rsqrt(ms + EPS) * (g * _sigmoid(g))
        o_ref[pl.ds(r0, chunk), :] = out[0:chunk].astype(o_ref.dtype)
        return new_state

    sfin_ref[0, 0] = lax.fori_loop(0, n_chunks, body, s0_ref[0, 0], unroll=math.gcd(n_chunks, 4))


def _retention(z, b, l, col0, h_ret, tabs, log_gamma_rows, s0):
    m = z.shape[0]
    chunk = RET_CHUNK if l % RET_CHUNK == 0 else l
    n_chunks = l // chunk
    cos, sin = tabs
    zs = lambda sec: pl.BlockSpec((l, HEAD_DIM), functools.partial(
        lambda bb, h, sec: (bb, col0 + sec * h_ret + h), sec=sec))
    tab_spec = pl.BlockSpec((l, HEAD_DIM), lambda bb, h: (0, 0))
    scratch = [] if chunk == RET_CHUNK else [pltpu.VMEM((RET_CHUNK, HEAD_DIM), F32)]
    return pl.pallas_call(
        functools.partial(_retention_kernel, chunk=chunk, n_chunks=n_chunks),
        out_shape=(jax.ShapeDtypeStruct((m, h_ret * HEAD_DIM), F32 if chunk < 16 else BF16),
                   jax.ShapeDtypeStruct((b, h_ret, HEAD_DIM, HEAD_DIM), F32)),
        grid=(b, h_ret),
        in_specs=[zs(0), zs(1), zs(2), zs(3), tab_spec, tab_spec,
                  pl.BlockSpec((1, 1, HEAD_DIM), lambda bb, h: (h, 0, 0)),
                  pl.BlockSpec((1, 1, HEAD_DIM, HEAD_DIM), lambda bb, h: (bb, h, 0, 0))],
        out_specs=(pl.BlockSpec((l, HEAD_DIM), lambda bb, h: (bb, h)),
                   pl.BlockSpec((1, 1, HEAD_DIM, HEAD_DIM), lambda bb, h: (bb, h, 0, 0))),
        scratch_shapes=scratch,
        compiler_params=_params(("parallel", "parallel")),
        name="retention",
    )(z, z, z, z, cos, sin, log_gamma_rows, s0)


def _s5_kernel(u_ref, bb_ref, cc_ref, d_ref, are_ref, aim_ref, h0_ref, y_ref, xfin_ref, ui_s, bu_s, x_s,
               *, steps, n_seq):
    i = pl.program_id(1)
    nc = S5_STATE_COLS

    @pl.when(i == 0)
    def _():
        x_s[...] = h0_ref[0]

    if n_seq < SUBLANES:
        ui_s[...] = jnp.zeros(ui_s.shape, F32)
    for s in range(n_seq):
        ui_s[pl.ds(s, steps, stride=SUBLANES), :] = u_ref[s]
    u = ui_s[...]
    bu_s[...] = jnp.dot(u.astype(BF16), bb_ref[0], preferred_element_type=F32)
    a_re = jnp.broadcast_to(are_ref[0], (SUBLANES, nc))
    a_im = jnp.broadcast_to(aim_ref[0], (SUBLANES, nc))

    def body(t, carry):
        xr, xi = carry
        r0 = pl.multiple_of(t * SUBLANES, SUBLANES)
        nr = a_re * xr - a_im * xi + bu_s[pl.ds(r0, SUBLANES), 0:nc]
        ni = a_re * xi + a_im * xr + bu_s[pl.ds(r0, SUBLANES), nc:2 * nc]
        bu_s[pl.ds(r0, SUBLANES), 0:nc] = nr
        bu_s[pl.ds(r0, SUBLANES), nc:2 * nc] = ni
        return nr, ni

    xr, xi = lax.fori_loop(0, steps, body, (x_s[:, 0:nc], x_s[:, nc:2 * nc]), unroll=min(8, steps))
    x_s[:, 0:nc] = xr
    x_s[:, nc:2 * nc] = xi
    ui_s[...] = jnp.dot(bu_s[...].astype(BF16), cc_ref[0], preferred_element_type=F32) + d_ref[...] * u
    for s in range(n_seq):
        y_ref[s] = ui_s[pl.ds(s, steps, stride=SUBLANES), :]

    @pl.when(i == pl.num_programs(1) - 1)
    def _():
        xfin_ref[0] = x_s[...]


def _s5(z3, col_block0, w_ssm, bb, cc, d_row, a_re, a_im, h0):
    n_seq, n_steps, _ = z3.shape
    n_blocks = w_ssm // LANES
    t_chunk = _pick_tile(n_steps, (256, 128, 64, 32, 16, 8))
    r_chunk = t_chunk * SUBLANES
    return pl.pallas_call(
        functools.partial(_s5_kernel, steps=t_chunk, n_seq=n_seq),
        out_shape=(jax.ShapeDtypeStruct((n_seq, n_steps, w_ssm), F32),
                   jax.ShapeDtypeStruct((n_blocks, SUBLANES, 2 * S5_STATE_COLS), F32)),
        grid=(n_blocks, n_steps // t_chunk),
        in_specs=[pl.BlockSpec((n_seq, t_chunk, LANES), lambda j, i: (0, i, col_block0 + j)),
                  pl.BlockSpec((1, LANES, 2 * S5_STATE_COLS), lambda j, i: (j, 0, 0)),
                  pl.BlockSpec((1, 2 * S5_STATE_COLS, LANES), lambda j, i: (j, 0, 0)),
                  pl.BlockSpec((1, LANES), lambda j, i: (0, j)),
                  pl.BlockSpec((1, 1, S5_STATE_COLS), lambda j, i: (j, 0, 0)),
                  pl.BlockSpec((1, 1, S5_STATE_COLS), lambda j, i: (j, 0, 0)),
                  pl.BlockSpec((1, SUBLANES, 2 * S5_STATE_COLS), lambda j, i: (j, 0, 0))],
        out_specs=(pl.BlockSpec((n_seq, t_chunk, LANES), lambda j, i: (0, i, j)),
                   pl.BlockSpec((1, SUBLANES, 2 * S5_STATE_COLS), lambda j, i: (j, 0, 0))),
        scratch_shapes=[pltpu.VMEM((r_chunk, LANES), F32),
                        pltpu.VMEM((r_chunk, 2 * S5_STATE_COLS), F32),
                        pltpu.VMEM((SUBLANES, 2 * S5_STATE_COLS), F32)],
        compiler_params=_params(("parallel", "arbitrary")),
        name="s5_scan",
    )(z3, bb, cc, d_row, a_re, a_im, h0)


def _s5_post_kernel(y_ref, w_ref, g_ref, o_ref):
    y = y_ref[...]
    y = 0.5 * y * (1.0 + jnp.tanh(math.sqrt(2.0 / math.pi) * (y + 0.044715 * (y * y * y))))
    gate = _sigmoid(jnp.dot(y.astype(BF16), w_ref[...], preferred_element_type=F32))
    y = y * gate
    ms = jnp.mean(y * y, axis=-1, keepdims=True)
    o_ref[...] = (y * lax.rsqrt(ms + EPS) * g_ref[...]).astype(o_ref.dtype)


def _s5_post(y, w_glu, g):
    m, w = y.shape
    tm = _pick_tile(m, (512, 256, 128, 64))
    return pl.pallas_call(
        _s5_post_kernel,
        out_shape=jax.ShapeDtypeStruct((m, w), BF16),
        grid=(m // tm,),
        in_specs=[pl.BlockSpec((tm, w), lambda i: (i, 0)), pl.BlockSpec((w, w), lambda i: (0, 0)),
                  pl.BlockSpec((1, w), lambda i: (0, 0))],
        out_specs=pl.BlockSpec((tm, w), lambda i: (i, 0)),
        compiler_params=_params(("parallel",)),
        name="s5_gelu_glu_norm",
    )(y, w_glu, g.reshape(1, w))


def _s5_discretise(lam_re, lam_im, log_dt, b_re, b_im, c_re, c_im):
    g_cnt = lam_re.shape[0]
    dt = jnp.exp(log_dt.astype(F32))[:, None]
    mag = jnp.exp(lam_re * dt)
    ang = lam_im * dt
    a_re, a_im = mag * jnp.cos(ang), mag * jnp.sin(ang)
    den = lam_re * lam_re + lam_im * lam_im
    num_re = a_re - 1.0
    coef_re = (num_re * lam_re + a_im * lam_im) / den
    coef_im = (a_im * lam_re - num_re * lam_im) / den
    bb_re = coef_re[..., None] * b_re - coef_im[..., None] * b_im
    bb_im = coef_re[..., None] * b_im + coef_im[..., None] * b_re
    nb = g_cnt // S5_GROUPS_PER_BLOCK
    gb = S5_GROUPS_PER_BLOCK
    eye = jnp.eye(gb, dtype=F32)

    def block_diag_in(w):
        w = w.reshape(nb, gb, SSM_P, SSM_H)
        return jnp.einsum('ngph,gk->nghkp', w, eye).reshape(nb, gb * SSM_H, gb * SSM_P)

    def block_diag_out(w):
        w = w.reshape(nb, gb, SSM_H, SSM_P)
        return jnp.einsum('nghp,gk->ngpkh', w, eye).reshape(nb, gb * SSM_P, gb * SSM_H)

    bb = jnp.concatenate([block_diag_in(bb_re), block_diag_in(bb_im)], axis=2).astype(BF16)
    cc = jnp.concatenate([block_diag_out(c_re.astype(F32)), block_diag_out(-c_im.astype(F32))], axis=1).astype(BF16)
    a_re_b = a_re.reshape(nb, 1, gb * SSM_P)
    a_im_b = a_im.reshape(nb, 1, gb * SSM_P)
    return bb, cc, a_re_b, a_im_b


def _s5_mixer(z, b, l, col0, w_ssm, prm, h0_re, h0_im):
    nb = w_ssm // LANES
    gb = S5_GROUPS_PER_BLOCK
    def pack_state(h):
        h = jnp.pad(h.astype(F32), ((0, SUBLANES - b), (0, 0), (0, 0)))
        return h.reshape(SUBLANES, nb, gb * SSM_P).transpose(1, 0, 2)

    h0 = jnp.concatenate([pack_state(h0_re), pack_state(h0_im)], axis=2)
    y, xfin = _s5(z.reshape(b, l, z.shape[1]), col0 // LANES, w_ssm,
                  prm['bb'], prm['cc'], prm['d'], prm['a_re'], prm['a_im'], h0)
    o = _s5_post(y.reshape(b * l, w_ssm), prm['w_glu'], prm['norm_g'])

    def unpack_state(x):
        return x.transpose(1, 0, 2).reshape(SUBLANES, nb * gb, SSM_P)[:b]

    return o, unpack_state(xfin[:, :, :S5_STATE_COLS]), unpack_state(xfin[:, :, S5_STATE_COLS:])


def _att_rope_tables(pos):
    freqs = ROPE_THETA ** (-jnp.arange(0, DK_HALF, 2, dtype=F32) / DK_HALF)
    ang = pos.astype(F32)[:, None] * freqs[None, :]
    c, s = jnp.cos(ang), jnp.sin(ang)
    zero = jnp.zeros_like(s)
    return (jnp.concatenate([c, c, c, c], axis=-1),
            jnp.concatenate([-s, zero, -s, zero], axis=-1),
            jnp.concatenate([zero, s, zero, s], axis=-1))


def _ret_rope_tables(pos):
    freqs = 1.0 / (ROPE_THETA ** jnp.linspace(0.0, 1.0, HEAD_DIM // 2, dtype=F32))
    ang = pos.astype(F32)[:, None] * freqs[None, :]
    c, s = jnp.cos(ang), jnp.sin(ang)
    return jnp.concatenate([c, c], axis=-1), jnp.concatenate([-s, s], axis=-1)


def _row128(v):
    return jnp.tile(v.astype(F32), HEAD_DIM // v.shape[0]).reshape(1, HEAD_DIM)


def _mixers(z, b, l, lw, attend, ret_tabs, ret_s0, s5_h0_re, s5_h0_im):
    d = lw['d_model']
    w_att = lw['w_att']
    o_att, new_k, new_v = attend(z)
    o_ret, s_ret = _retention(z, b, l, 3 * w_att // HEAD_DIM, w_att // HEAD_DIM, ret_tabs, lw['log_gamma'], ret_s0)
    o_ssm, s5_re, s5_im = _s5_mixer(z, b, l, 7 * w_att, d - 2 * w_att, lw['s5'], s5_h0_re, s5_h0_im)
    return [o_att, o_ret.astype(BF16), o_ssm], (new_k, new_v, s_ret, s5_re, s5_im)


def kernel(x_prompt, x_sample, cache_k, cache_v, page_table, state_ret, state_s5_re, state_s5_im,
           norm_mix_g, w_in, q_norm_g, k_norm_g, lambda_q1, lambda_k1, lambda_q2, lambda_k2, subln_g,
           s5_lambda_re, s5_lambda_im, s5_log_dt, s5_b_re, s5_b_im, s5_c_re, s5_c_im, s5_d, s5_w_glu,
           s5_norm_g, w_out, norm_ffn_g, w_gate, w_up, w_down):
    b_p, seq, d = x_prompt.shape
    b_s, n_new, _ = x_sample.shape
    depth = w_in.shape[0]
    w_att = (3 * d) // (8 * HEAD_DIM) * HEAD_DIM
    h_att = w_att // HEAD_DIM
    w_ssm = d - 2 * w_att
    n_pages = page_table.shape[1]
    past = n_pages * PAGE_SIZE
    assert (2 * w_att) % w_ssm == 0 and w_ssm % LANES == 0 and b_p <= SUBLANES and b_s <= SUBLANES
    ssm_rb = (2 * w_att) // w_ssm

    pos_p = jnp.arange(seq, dtype=jnp.int32)
    pos_s = past + jnp.arange(n_new, dtype=jnp.int32)
    att_tabs_p, att_tabs_s = _att_rope_tables(pos_p), _att_rope_tables(pos_s)
    ret_tabs_p, ret_tabs_s = _ret_rope_tables(pos_p), _ret_rope_tables(pos_s)
    log_gamma = jnp.log(1.0 - 2.0 ** (-5.0 - jnp.arange(h_att, dtype=F32)))
    log_gamma_rows = jnp.broadcast_to(log_gamma[:, None, None], (h_att, 1, HEAD_DIM))

    xp = x_prompt.reshape(b_p * seq, d)
    xs = x_sample.reshape(b_s * n_new, d)
    zeros_ret = jnp.zeros((b_p, h_att, HEAD_DIM, HEAD_DIM), F32)
    zeros_s5 = jnp.zeros((b_p, w_ssm // SSM_H, SSM_P), F32)
    w_down_b = w_down.astype(BF16)
    cache_kh = jnp.transpose(cache_k, (0, 1, 3, 2, 4))
    cache_vh = jnp.transpose(cache_v, (0, 1, 3, 2, 4))
    outs = [[] for _ in range(8)]
    kv_prompt = None
    for layer in range(depth):
        lam_init = 0.8 - 0.6 * math.exp(-0.3 * layer)
        lam = (jnp.exp(jnp.sum(lambda_q1[layer].astype(F32) * lambda_k1[layer].astype(F32)))
               - jnp.exp(jnp.sum(lambda_q2[layer].astype(F32) * lambda_k2[layer].astype(F32))) + lam_init)
        lam_row = jnp.broadcast_to(lam.reshape(1, 1), (1, HEAD_DIM)).astype(F32)
        bb, cc, a_re, a_im = _s5_discretise(
            s5_lambda_re[layer].astype(F32), s5_lambda_im[layer].astype(F32), s5_log_dt[layer],
            s5_b_re[layer].astype(F32), s5_b_im[layer].astype(F32), s5_c_re[layer], s5_c_im[layer])
        lw = {
            'd_model': d, 'w_att': w_att, 'log_gamma': log_gamma_rows,
            's5': {'bb': bb, 'cc': cc, 'a_re': a_re, 'a_im': a_im,
                   'd': s5_d[layer].astype(F32).reshape(1, w_ssm),
                   'w_glu': s5_w_glu[layer].astype(BF16), 'norm_g': s5_norm_g[layer].astype(F32)},
        }
        qg, kg, sg = _row128(q_norm_g[layer]), _row128(k_norm_g[layer]), _row128(subln_g[layer])
        out_scale = 1.0 - lam_init

        hp, hs = _rmsnorm(xp, norm_mix_g[layer]), _rmsnorm(xs, norm_mix_g[layer])
        zp, zs = _matmul_pair([hp], [hs], w_in, layer, [0], tm=1024, tn=512, name="in_proj")

        attend_p = functools.partial(_prompt_attention, b=b_p, l=seq, w_att=w_att, tabs=att_tabs_p,
                                     qg=qg, kg=kg, sg=sg, lam_row=lam_row, out_scale=out_scale,
                                     layer=layer, depth=depth, kv_prev=kv_prompt)
        mix_p, (kp, vp, rp, p_re, p_im) = _mixers(zp, b_p, seq, lw, attend_p, ret_tabs_p,
                                                  zeros_ret, zeros_s5, zeros_s5)
        kv_prompt = (kp, vp)

        def attend_s(z):
            o, nk, nv = _sample_attention(z.reshape(b_s, n_new, z.shape[1]), cache_kh, cache_vh, layer,
                                          page_table, w_att, att_tabs_s, qg, kg, sg, lam_row, out_scale)
            flat = lambda t: t.reshape(b_s * n_new, w_att)
            return flat(o).astype(BF16), flat(nk), flat(nv)

        mix_s, (k_s, v_s, r_s, s_re, s_im) = _mixers(zs, b_s, n_new, lw, attend_s, ret_tabs_s, state_ret[layer],
                                                     state_s5_re[layer], state_s5_im[layer])

        xp, xs = _matmul_pair(mix_p, mix_s, w_out, layer, [0, 1, ssm_rb], mode="res", res_p=xp, res_s=xs,
                              tm=1024, tn=512, name="out_proj")
        hp, hs = _rmsnorm(xp, norm_ffn_g[layer]), _rmsnorm(xs, norm_ffn_g[layer])
        act_p, act_s = _matmul_pair([hp], [hs], w_gate, layer, [0], mode="swiglu", w2=w_up,
                                    out_dtype=BF16, tm=1024, tn=256, name="ffn_gate_up")
        xp = _matmul_res(act_p, w_down_b, layer, xp, tm=512, tn=512, name="ffn_down")
        xs = _matmul_res(act_s, w_down_b, layer, xs, tm=512, tn=512, name="ffn_down")

        for lst, val in zip(outs, (k_s.reshape(b_s, n_new, h_att, HEAD_DIM), v_s.reshape(b_s, n_new, h_att, HEAD_DIM),
                                   rp, r_s, p_re, p_im, s_re, s_im)):
            lst.append(val)
    kv_out = tuple(jnp.transpose(t, (0, 1, 3, 2, 4)) for t in kv_prompt)
    return (xp.reshape(b_p, seq, d), xs.reshape(b_s, n_new, d)) + kv_out + tuple(jnp.stack(o) for o in outs)
```

```python
import functools
import math

import jax
import jax.numpy as jnp
from jax import lax
from jax.experimental import pallas as pl
from jax.experimental.pallas import tpu as pltpu

F32 = jnp.float32
BF16 = jnp.bfloat16

HEAD_DIM = 128
DK_HALF = HEAD_DIM // 2
SSM_H = 16
SSM_P = 64
PAGE_SIZE = 128
RET_CHUNK = 128
ROPE_THETA = 10000.0
EPS = 1e-6
LANES = 128
SUBLANES = 8
S5_GROUPS_PER_BLOCK = LANES // SSM_H
S5_STATE_COLS = S5_GROUPS_PER_BLOCK * SSM_P
NEG_BIG = -1e30
V7X_VMEM_LIMIT_BYTES = 56 * 1024 * 1024


def _pick_tile(n, candidates):
    for c in candidates:
        if n % c == 0:
            return c
    return n


def _params(semantics):
    return pltpu.CompilerParams(dimension_semantics=semantics, vmem_limit_bytes=V7X_VMEM_LIMIT_BYTES)


def _sigmoid(x):
    return 1.0 / (1.0 + jnp.exp(-x))


def _rmsnorm_kernel(x_ref, g_ref, o_ref):
    x = x_ref[...]
    ms = jnp.mean(x * x, axis=-1, keepdims=True)
    o_ref[...] = (x * lax.rsqrt(ms + EPS) * g_ref[...]).astype(o_ref.dtype)


def _rmsnorm(x, g):
    m, d = x.shape
    tm = _pick_tile(m, (512, 256, 128, 64))
    return pl.pallas_call(
        _rmsnorm_kernel,
        out_shape=jax.ShapeDtypeStruct((m, d), BF16),
        grid=(m // tm,),
        in_specs=[pl.BlockSpec((tm, d), lambda i: (i, 0)), pl.BlockSpec((1, d), lambda i: (0, 0))],
        out_specs=pl.BlockSpec((tm, d), lambda i: (i, 0)),
        compiler_params=_params(("parallel",)),
        name="rmsnorm",
    )(x, g.reshape(1, d))


def _mm_res_kernel(a_ref, b_ref, res_ref, o_ref):
    o_ref[...] = res_ref[...] + jnp.dot(a_ref[...], b_ref[...], preferred_element_type=F32)


def _matmul_res(a, w_all, layer, res, *, tm, tn, name):
    m, k = a.shape
    n = w_all.shape[2]
    tm = _pick_tile(m, (tm, 512, 256, 128, 64))
    tn = _pick_tile(n, (tn, 512, 256, 128))
    return pl.pallas_call(
        _mm_res_kernel,
        out_shape=jax.ShapeDtypeStruct((m, n), F32),
        grid=(m // tm, n // tn),
        in_specs=[pl.BlockSpec((tm, k), lambda i, j: (i, 0)),
                  pl.BlockSpec((None, k, tn), lambda i, j: (layer, 0, j)),
                  pl.BlockSpec((tm, tn), lambda i, j: (i, j))],
        out_specs=pl.BlockSpec((tm, tn), lambda i, j: (i, j)),
        compiler_params=_params(("parallel", "arbitrary")),
        name=name,
    )(a, w_all, res)


def _mm_pair_kernel(*refs, n_a, n_w, mode):
    ap_refs, as_refs = refs[:n_a], refs[n_a:2 * n_a]
    w_refs = refs[2 * n_a:2 * n_a + n_w]
    pos = 2 * n_a + n_w
    res_refs = refs[pos:pos + 2] if mode == "res" else ()
    pos += len(res_refs)
    op_ref, os_ref = refs[pos], refs[pos + 1]
    wb_refs = refs[pos + 2:]
    first = pl.program_id(1) == 0

    @pl.when(first)
    def _():
        for w_ref, wb_ref in zip(w_refs, wb_refs):
            wb_ref[...] = w_ref[...].astype(BF16)

    def product(a_vals):
        if mode == "swiglu":
            g = jnp.dot(a_vals[0], wb_refs[0][...], preferred_element_type=F32)
            u = jnp.dot(a_vals[0], wb_refs[1][...], preferred_element_type=F32)
            return g * _sigmoid(g) * u
        acc = jnp.dot(a_vals[0], wb_refs[0][...], preferred_element_type=F32)
        for a, wb_ref in zip(a_vals[1:], wb_refs[1:]):
            acc = acc + jnp.dot(a, wb_ref[...], preferred_element_type=F32)
        return acc

    def store(o_ref, val, res_ref):
        o_ref[...] = (val if res_ref is None else res_ref[...] + val).astype(o_ref.dtype)

    res_p, res_s = res_refs if res_refs else (None, None)
    tm = op_ref.shape[0]

    @pl.when(first)
    def _():
        both = product([jnp.concatenate([ap[...], as_[...]], axis=0) for ap, as_ in zip(ap_refs, as_refs)])
        store(op_ref, both[:tm], res_p)
        store(os_ref, both[tm:], res_s)

    @pl.when(jnp.logical_not(first))
    def _():
        store(op_ref, product([ap[...] for ap in ap_refs]), res_p)


def _matmul_pair(ap_list, as_list, w, layer, row_blocks, *, mode="plain", res_p=None, res_s=None, w2=None,
                 out_dtype=F32, tm, tn, name):
    mp, ms = ap_list[0].shape[0], as_list[0].shape[0]
    n = w.shape[2]
    tm = _pick_tile(mp, (tm, 512, 256, 128, 64))
    tn = _pick_tile(n, (tn, 512, 256, 128))
    in_specs = [pl.BlockSpec((tm, a.shape[1]), lambda j, i: (i, 0)) for a in ap_list]
    in_specs += [pl.BlockSpec((ms, a.shape[1]), lambda j, i: (0, 0)) for a in as_list]
    args = list(ap_list) + list(as_list)
    scratch = []
    weights = [(w, a.shape[1], rb) for a, rb in zip(ap_list, row_blocks)]
    if mode == "swiglu":
        weights.append((w2, ap_list[0].shape[1], 0))
    for wt, rows, rb in weights:
        in_specs.append(pl.BlockSpec((None, rows, tn), functools.partial(lambda j, i, rb: (layer, rb, j), rb=rb)))
        args.append(wt)
        scratch.append(pltpu.VMEM((rows, tn), BF16))
    if mode == "res":
        in_specs += [pl.BlockSpec((tm, tn), lambda j, i: (i, j)), pl.BlockSpec((ms, tn), lambda j, i: (0, j))]
        args += [res_p, res_s]
    return pl.pallas_call(
        functools.partial(_mm_pair_kernel, n_a=len(ap_list), n_w=len(weights), mode=mode),
        out_shape=(jax.ShapeDtypeStruct((mp, n), out_dtype), jax.ShapeDtypeStruct((ms, n), out_dtype)),
        grid=(n // tn, mp // tm),
        in_specs=in_specs,
        out_specs=(pl.BlockSpec((tm, tn), lambda j, i: (i, j)), pl.BlockSpec((ms, tn), lambda j, i: (0, j))),
        scratch_shapes=scratch,
        compiler_params=_params(("parallel", "arbitrary")),
        name=name,
    )(*args)


def _half_rmsnorm(x, g):
    lane = lax.broadcasted_iota(jnp.int32, x.shape, 1)
    first = lane < DK_HALF
    sq = x * x
    tot = jnp.sum(sq, axis=-1, keepdims=True)
    s1 = jnp.sum(jnp.where(first, sq, 0.0), axis=-1, keepdims=True)
    ms = jnp.where(first, s1, tot - s1) * (1.0 / DK_HALF)
    return x * lax.rsqrt(ms + EPS) * g


def _att_rope(x, cos, sin_a, sin_b):
    return x * cos + pltpu.roll(x, LANES - DK_HALF // 2, 1) * sin_a + pltpu.roll(x, DK_HALF // 2, 1) * sin_b


def _ret_rope(x, cos, sin_signed):
    return x * cos + pltpu.roll(x, DK_HALF, 1) * sin_signed


def _stack_half_masked(q):
    lane = lax.broadcasted_iota(jnp.int32, q.shape, 1)
    first = lane < DK_HALF
    return jnp.concatenate([jnp.where(first, q, 0.0), jnp.where(first, 0.0, q)], axis=0)


def _sub_layer_norm(o, g, out_scale):
    ms = jnp.mean(o * o, axis=-1, keepdims=True)
    return o * lax.rsqrt(ms + EPS) * g * out_scale


def _prompt_attn_kernel(zq_ref, zk_ref, zv_ref, cq_ref, saq_ref, sbq_ref, ck_ref, sak_ref, sbk_ref,
                        qg_ref, kg_ref, sg_ref, lam_ref,
                        o_ref, newk_ref, newv_ref,
                        k_s, vt_s, q2_s, m_s, l_s, acc_s, *, tq, n_tiles, n_chains, out_scale):
    i = pl.program_id(2)

    @pl.when(i == 0)
    def _():
        k = _att_rope(_half_rmsnorm(zk_ref[...], kg_ref[...]), ck_ref[...], sak_ref[...], sbk_ref[...])
        newk_ref[...] = k
        k_s[...] = k.astype(BF16)
        v = zv_ref[...]
        newv_ref[...] = v
        for t in range(n_tiles):
            vt_s[t] = v[t * tq:(t + 1) * tq].T.astype(BF16)

    q = _att_rope(_half_rmsnorm(zq_ref[...], qg_ref[...]), cq_ref[...], saq_ref[...], sbq_ref[...])
    q2_s[...] = _stack_half_masked(q * (DK_HALF ** -0.5)).astype(BF16)
    m_s[...] = jnp.full(m_s.shape, NEG_BIG, F32)
    l_s[...] = jnp.zeros(l_s.shape, F32)
    acc_s[...] = jnp.zeros(acc_s.shape, F32)

    wc = (2 * tq) // n_chains

    def block(j, masked):
        kj = k_s[pl.ds(pl.multiple_of(j * tq, tq), tq), :]
        vj = vt_s[j]
        for c in range(n_chains):
            cols = slice(c * wc, (c + 1) * wc)
            s = lax.dot_general(kj, q2_s[cols, :], (((1,), (1,)), ((), ())), preferred_element_type=F32)
            if masked:
                key = lax.broadcasted_iota(jnp.int32, s.shape, 0)
                qry = lax.broadcasted_iota(jnp.int32, s.shape, 1) + c * wc
                qry = jnp.where(qry >= tq, qry - tq, qry)
                s = jnp.where(key <= qry, s, NEG_BIG)
            m_prev = m_s[:, cols]
            m_new = jnp.maximum(m_prev, jnp.max(s, axis=0, keepdims=True))
            alpha = jnp.exp(m_prev - m_new)
            p = jnp.exp(s - m_new)
            l_s[:, cols] = alpha * l_s[:, cols] + jnp.sum(p, axis=0, keepdims=True)
            acc_s[:, cols] = alpha * acc_s[:, cols] + jnp.dot(vj, p.astype(BF16), preferred_element_type=F32)
            m_s[:, cols] = m_new

    def body(j, carry):
        block(j, False)
        return carry

    lax.fori_loop(0, i, body, 0)
    block(i, True)

    o_full = acc_s[...] / l_s[...]
    o_t = o_full[:, :tq] - lam_ref[0:1, 0:1] * o_full[:, tq:]
    o_ref[...] = _sub_layer_norm(o_t.T, sg_ref[...], out_scale).astype(o_ref.dtype)


def _prompt_attn_kernel_aliased(*refs, n_in, **kw):
    _prompt_attn_kernel(*refs[:n_in], *refs[n_in + 2:], **kw)


def _prompt_attention(z, b, l, w_att, tabs, qg, kg, sg, lam_row, out_scale, layer, depth, kv_prev):
    m = z.shape[0]
    h_att = w_att // HEAD_DIM
    tq = _pick_tile(l, (512, 256, 128))
    nq = l // tq
    cos, sa, sb = tabs
    q_spec = pl.BlockSpec((tq, HEAD_DIM), lambda bb, h, i: (bb * nq + i, h))
    k_spec = pl.BlockSpec((l, HEAD_DIM), lambda bb, h, i: (bb, h_att + h))
    v_spec = pl.BlockSpec((l, HEAD_DIM), lambda bb, h, i: (bb, 2 * h_att + h))
    tq_spec = pl.BlockSpec((tq, HEAD_DIM), lambda bb, h, i: (i, 0))
    tk_spec = pl.BlockSpec((l, HEAD_DIM), lambda bb, h, i: (0, 0))
    row_spec = pl.BlockSpec((1, HEAD_DIM), lambda bb, h, i: (0, 0))
    kv_out_spec = pl.BlockSpec((None, None, None, l, HEAD_DIM), lambda bb, h, i: (layer, bb, h, 0, 0))
    in_specs = [q_spec, k_spec, v_spec, tq_spec, tq_spec, tq_spec, tk_spec, tk_spec, tk_spec,
                row_spec, row_spec, row_spec, row_spec]
    args = [z, z, z, cos, sa, sb, cos, sa, sb, qg, kg, sg, lam_row]
    kw = dict(tq=tq, n_tiles=nq, n_chains=1, out_scale=out_scale)
    if kv_prev is None:
        body, aliases = functools.partial(_prompt_attn_kernel, **kw), {}
    else:
        body = functools.partial(_prompt_attn_kernel_aliased, n_in=len(args), **kw)
        aliases = {len(args): 1, len(args) + 1: 2}
        in_specs = in_specs + [pl.BlockSpec(memory_space=pl.ANY)] * 2
        args = args + list(kv_prev)
    kv_shape = jax.ShapeDtypeStruct((depth, b, h_att, l, HEAD_DIM), F32)
    return pl.pallas_call(
        body,
        out_shape=(jax.ShapeDtypeStruct((m, w_att), BF16), kv_shape, kv_shape),
        grid=(b, h_att, nq),
        in_specs=in_specs,
        out_specs=(q_spec, kv_out_spec, kv_out_spec),
        input_output_aliases=aliases,
        scratch_shapes=[pltpu.VMEM((l, HEAD_DIM), BF16), pltpu.VMEM((nq, HEAD_DIM, tq), BF16),
                        pltpu.VMEM((2 * tq, HEAD_DIM), BF16),
                        pltpu.VMEM((1, 2 * tq), F32), pltpu.VMEM((1, 2 * tq), F32),
                        pltpu.VMEM((HEAD_DIM, 2 * tq), F32)],
        compiler_params=_params(("parallel", "parallel", "arbitrary")),
        name="prompt_diff_attention",
    )(*args)


def _sample_attn_kernel(*refs, h_att, n_new, pages_per_step, out_scale):
    g_cnt = pages_per_step
    zq_ref, zk_ref, zv_ref = refs[1:4]
    kc_refs = refs[4:4 + g_cnt]
    vc_refs = refs[4 + g_cnt:4 + 2 * g_cnt]
    cos_ref, sa_ref, sb_ref, qg_ref, kg_ref, sg_ref, lam_ref = refs[4 + 2 * g_cnt:11 + 2 * g_cnt]
    o_ref, newk_ref, newv_ref, q2_s, kn_s, vn_s, m_s, l_s, acc_s = refs[11 + 2 * g_cnt:]
    p = pl.program_id(1)
    rows_h = 2 * n_new

    @pl.when(p == 0)
    def _():
        kn_s[...] = jnp.zeros(kn_s.shape, F32)
        vn_s[...] = jnp.zeros(vn_s.shape, F32)
        for h in range(h_att):
            cs = slice(h * HEAD_DIM, (h + 1) * HEAD_DIM)
            q = _att_rope(_half_rmsnorm(zq_ref[0, :, cs], qg_ref[...]), cos_ref[...], sa_ref[...], sb_ref[...])
            q2_s[h * rows_h:(h + 1) * rows_h, :] = _stack_half_masked(q * (DK_HALF ** -0.5)).astype(BF16)
            k = _att_rope(_half_rmsnorm(zk_ref[0, :, cs], kg_ref[...]), cos_ref[...], sa_ref[...], sb_ref[...])
            newk_ref[0, :, cs] = k
            kn_s[0:n_new, cs] = k
        v = zv_ref[0]
        newv_ref[0] = v
        vn_s[0:n_new, :] = v
        m_s[...] = jnp.full(m_s.shape, NEG_BIG, F32)
        l_s[...] = jnp.zeros(l_s.shape, F32)
        acc_s[...] = jnp.zeros(acc_s.shape, F32)

    def attend(k_pages, v_pages, mask):
        s = jnp.concatenate(
            [jnp.concatenate(
                [lax.dot_general(q2_s[h * rows_h:(h + 1) * rows_h, :], kp[h], (((1,), (1,)), ((), ())),
                                 preferred_element_type=F32) for kp in k_pages], axis=1)
             for h in range(h_att)], axis=0)
        if mask is not None:
            s = jnp.where(mask, s, NEG_BIG)
        m_prev = m_s[...]
        m_new = jnp.maximum(m_prev, jnp.max(s, axis=-1, keepdims=True))
        alpha = jnp.exp(m_prev - m_new)
        pr = jnp.exp(s - m_new)
        l_s[...] = alpha * l_s[...] + jnp.sum(pr, axis=-1, keepdims=True)
        pb = pr.astype(BF16)
        pv = []
        for h in range(h_att):
            acc = None
            for g, vp in enumerate(v_pages):
                part = jnp.dot(pb[h * rows_h:(h + 1) * rows_h, g * PAGE_SIZE:(g + 1) * PAGE_SIZE], vp[h],
                               preferred_element_type=F32)
                acc = part if acc is None else acc + part
            pv.append(acc)
        acc_s[...] = alpha * acc_s[...] + jnp.concatenate(pv, axis=0)
        m_s[...] = m_new

    heads = lambda r: [r[h].astype(BF16) for h in range(h_att)]
    attend([heads(r) for r in kc_refs], [heads(r) for r in vc_refs], None)

    @pl.when(p == pl.num_programs(1) - 1)
    def _():
        shape = (h_att * rows_h, PAGE_SIZE)
        q_tok = lax.rem(lax.broadcasted_iota(jnp.int32, shape, 0), n_new)
        key = lax.broadcasted_iota(jnp.int32, shape, 1)
        head_cols = lambda ref: [ref[:, h * HEAD_DIM:(h + 1) * HEAD_DIM].astype(BF16) for h in range(h_att)]
        attend([head_cols(kn_s)], [head_cols(vn_s)], key <= q_tok)
        o_full = acc_s[...] / l_s[...]
        for h in range(h_att):
            o1 = o_full[h * rows_h:h * rows_h + n_new]
            o2 = o_full[h * rows_h + n_new:(h + 1) * rows_h]
            o = o1 - lam_ref[...] * o2
            o_ref[0, :, h * HEAD_DIM:(h + 1) * HEAD_DIM] = _sub_layer_norm(o, sg_ref[...], out_scale)


def _sample_attention(z3, cache_k, cache_v, layer, page_table, w_att, tabs, qg, kg, sg, lam_row, out_scale):
    bsz, n_new, _ = z3.shape
    n_pages = page_table.shape[1]
    h_att = w_att // HEAD_DIM
    g_cnt = _pick_tile(n_pages, (8, 4, 2, 1))
    cos, sa, sb = tabs
    z_spec = lambda c: pl.BlockSpec((1, n_new, w_att), functools.partial(lambda b, p, pt, c: (b, 0, c), c=c))
    cache_specs = [pl.BlockSpec((None, None, h_att, PAGE_SIZE, HEAD_DIM),
                                functools.partial(lambda b, p, pt, g: (layer, pt[b, p * g_cnt + g], 0, 0, 0), g=g))
                   for g in range(g_cnt)]
    tab_spec = pl.BlockSpec((n_new, HEAD_DIM), lambda b, p, pt: (0, 0))
    row_spec = pl.BlockSpec((1, HEAD_DIM), lambda b, p, pt: (0, 0))
    out_spec = pl.BlockSpec((1, n_new, w_att), lambda b, p, pt: (b, 0, 0))
    rows = h_att * 2 * n_new
    grid_spec = pltpu.PrefetchScalarGridSpec(
        num_scalar_prefetch=1,
        grid=(bsz, n_pages // g_cnt),
        in_specs=[z_spec(0), z_spec(1), z_spec(2)] + cache_specs + cache_specs
                 + [tab_spec, tab_spec, tab_spec, row_spec, row_spec, row_spec, row_spec],
        out_specs=(out_spec, out_spec, out_spec),
        scratch_shapes=[pltpu.VMEM((rows, HEAD_DIM), BF16),
                        pltpu.VMEM((PAGE_SIZE, w_att), F32), pltpu.VMEM((PAGE_SIZE, w_att), F32),
                        pltpu.VMEM((rows, 1), F32), pltpu.VMEM((rows, 1), F32),
                        pltpu.VMEM((rows, HEAD_DIM), F32)],
    )
    return pl.pallas_call(
        functools.partial(_sample_attn_kernel, h_att=h_att, n_new=n_new, pages_per_step=g_cnt,
                          out_scale=out_scale),
        out_shape=(jax.ShapeDtypeStruct((bsz, n_new, w_att), F32),) * 3,
        grid_spec=grid_spec,
        compiler_params=_params(("parallel", "arbitrary")),
        name="sample_paged_diff_attention",
    )(page_table, z3, z3, z3, *([cache_k] * g_cnt), *([cache_v] * g_cnt), cos, sa, sb, qg, kg, sg, lam_row)


def _retention_kernel(zq_ref, zk_ref, zv_ref, zg_ref, cos_ref, sin_ref, lg_ref, s0_ref,
                      o_ref, sfin_ref, *pad_s, chunk, n_chunks, heads):
    cp = RET_CHUNK if chunk < RET_CHUNK else chunk
    ri = lax.broadcasted_iota(jnp.int32, (cp, cp), 0)
    ci = lax.broadcasted_iota(jnp.int32, (cp, cp), 1)
    diff = (ri - ci).astype(F32)
    idx = lax.broadcasted_iota(jnp.int32, (cp, HEAD_DIM), 0).astype(F32)

    def load(ref, r0, cols):
        x = ref[pl.ds(r0, chunk), cols]
        if cp == chunk:
            return x
        buf = pad_s[0]
        buf[...] = jnp.zeros(buf.shape, F32)
        buf[0:chunk, :] = x
        return buf[...]

    for hh in range(heads):
        cols = slice(hh * HEAD_DIM, (hh + 1) * HEAD_DIM)
        tab = slice(0, HEAD_DIM)
        lg = lg_ref[hh]
        dec = jnp.where(diff >= 0, jnp.exp(jnp.maximum(diff, 0.0) * lg), 0.0)
        cross = jnp.exp((idx + 1.0) * lg)
        kdec_w = jnp.exp((chunk - 1.0 - idx) * lg)
        chunk_decay = jnp.exp(chunk * lg)

        def body(c, state, cols=cols, dec=dec, cross=cross, kdec_w=kdec_w, chunk_decay=chunk_decay):
            r0 = pl.multiple_of(c * chunk, chunk)
            cos = load(cos_ref, r0, tab)
            sin = load(sin_ref, r0, tab)
            q = _ret_rope(load(zq_ref, r0, cols), cos, sin)
            k = _ret_rope(load(zk_ref, r0, cols), cos, sin) * (HEAD_DIM ** -0.5)
            v = load(zv_ref, r0, cols)
            qb, kb, vb = q.astype(BF16), k.astype(BF16), v.astype(BF16)
            s = lax.dot_general(qb, kb, (((1,), (1,)), ((), ())), preferred_element_type=F32) * dec
            o = jnp.dot(s.astype(BF16), vb, preferred_element_type=F32)
            o = o + jnp.dot(qb, state.astype(BF16), preferred_element_type=F32) * cross
            kd = (k * kdec_w).astype(BF16)
            new_state = state * chunk_decay + lax.dot_general(kd, vb, (((0,), (0,)), ((), ())),
                                                              preferred_element_type=F32)
            ms = jnp.mean(o * o, axis=-1, keepdims=True)
            g = load(zg_ref, r0, cols)
            out = o * lax.rsqrt(ms + EPS) * (g * _sigmoid(g))
            o_ref[pl.ds(r0, chunk), cols] = out[0:chunk].astype(o_ref.dtype)
            return new_state

        sfin_ref[0, hh] = lax.fori_loop(0, n_chunks, body, s0_ref[0, hh], unroll=math.gcd(n_chunks, 4))


def _retention(z, b, l, col0, h_ret, tabs, log_gamma_rows, s0):
    m = z.shape[0]
    chunk = RET_CHUNK if l % RET_CHUNK == 0 else l
    n_chunks = l // chunk
    heads = h_ret if (chunk < RET_CHUNK and col0 % h_ret == 0) else 1
    cos, sin = tabs
    zs = lambda sec: pl.BlockSpec((l, heads * HEAD_DIM), functools.partial(
        lambda bb, h, sec: (bb, (col0 + sec * h_ret) // heads + h), sec=sec))
    tab_spec = pl.BlockSpec((l, HEAD_DIM), lambda bb, h: (0, 0))
    scratch = [] if chunk == RET_CHUNK else [pltpu.VMEM((RET_CHUNK, HEAD_DIM), F32)]
    return pl.pallas_call(
        functools.partial(_retention_kernel, chunk=chunk, n_chunks=n_chunks, heads=heads),
        out_shape=(jax.ShapeDtypeStruct((m, h_ret * HEAD_DIM), F32 if chunk < 16 else BF16),
                   jax.ShapeDtypeStruct((b, h_ret, HEAD_DIM, HEAD_DIM), F32)),
        grid=(b, h_ret // heads),
        in_specs=[zs(0), zs(1), zs(2), zs(3), tab_spec, tab_spec,
                  pl.BlockSpec((heads, 1, HEAD_DIM), lambda bb, h: (h, 0, 0)),
                  pl.BlockSpec((1, heads, HEAD_DIM, HEAD_DIM), lambda bb, h: (bb, h, 0, 0))],
        out_specs=(pl.BlockSpec((l, heads * HEAD_DIM), lambda bb, h: (bb, h)),
                   pl.BlockSpec((1, heads, HEAD_DIM, HEAD_DIM), lambda bb, h: (bb, h, 0, 0))),
        scratch_shapes=scratch,
        compiler_params=_params(("parallel", "parallel")),
        name="retention",
    )(z, z, z, z, cos, sin, log_gamma_rows, s0)


def _s5_kernel(u_ref, bb_ref, cc_ref, d_ref, are_ref, aim_ref, h0_ref, y_ref, xfin_ref, ui_s, bu_s, x_s,
               *, steps, n_seq):
    i = pl.program_id(1)
    nc = S5_STATE_COLS

    @pl.when(i == 0)
    def _():
        x_s[...] = h0_ref[0]

    if n_seq < SUBLANES:
        ui_s[...] = jnp.zeros(ui_s.shape, F32)
    for s in range(n_seq):
        ui_s[pl.ds(s, steps, stride=SUBLANES), :] = u_ref[s]
    u = ui_s[...]
    bu_s[...] = jnp.dot(u.astype(BF16), bb_ref[0], preferred_element_type=F32)
    a_re = jnp.broadcast_to(are_ref[0], (SUBLANES, nc))
    a_im = jnp.broadcast_to(aim_ref[0], (SUBLANES, nc))

    def body(t, carry):
        xr, xi = carry
        r0 = pl.multiple_of(t * SUBLANES, SUBLANES)
        nr = a_re * xr - a_im * xi + bu_s[pl.ds(r0, SUBLANES), 0:nc]
        ni = a_re * xi + a_im * xr + bu_s[pl.ds(r0, SUBLANES), nc:2 * nc]
        bu_s[pl.ds(r0, SUBLANES), 0:nc] = nr
        bu_s[pl.ds(r0, SUBLANES), nc:2 * nc] = ni
        return nr, ni

    xr, xi = lax.fori_loop(0, steps, body, (x_s[:, 0:nc], x_s[:, nc:2 * nc]), unroll=min(8, steps))
    x_s[:, 0:nc] = xr
    x_s[:, nc:2 * nc] = xi
    ui_s[...] = jnp.dot(bu_s[...].astype(BF16), cc_ref[0], preferred_element_type=F32) + d_ref[...] * u
    for s in range(n_seq):
        y_ref[s] = ui_s[pl.ds(s, steps, stride=SUBLANES), :]

    @pl.when(i == pl.num_programs(1) - 1)
    def _():
        xfin_ref[0] = x_s[...]


def _s5(z3, col_block0, w_ssm, bb, cc, d_row, a_re, a_im, h0):
    n_seq, n_steps, _ = z3.shape
    n_blocks = w_ssm // LANES
    t_chunk = _pick_tile(n_steps, (256, 128, 64, 32, 16, 8))
    r_chunk = t_chunk * SUBLANES
    return pl.pallas_call(
        functools.partial(_s5_kernel, steps=t_chunk, n_seq=n_seq),
        out_shape=(jax.ShapeDtypeStruct((n_seq, n_steps, w_ssm), F32),
                   jax.ShapeDtypeStruct((n_blocks, SUBLANES, 2 * S5_STATE_COLS), F32)),
        grid=(n_blocks, n_steps // t_chunk),
        in_specs=[pl.BlockSpec((n_seq, t_chunk, LANES), lambda j, i: (0, i, col_block0 + j)),
                  pl.BlockSpec((1, LANES, 2 * S5_STATE_COLS), lambda j, i: (j, 0, 0)),
                  pl.BlockSpec((1, 2 * S5_STATE_COLS, LANES), lambda j, i: (j, 0, 0)),
                  pl.BlockSpec((1, LANES), lambda j, i: (0, j)),
                  pl.BlockSpec((1, 1, S5_STATE_COLS), lambda j, i: (j, 0, 0)),
                  pl.BlockSpec((1, 1, S5_STATE_COLS), lambda j, i: (j, 0, 0)),
                  pl.BlockSpec((1, SUBLANES, 2 * S5_STATE_COLS), lambda j, i: (j, 0, 0))],
        out_specs=(pl.BlockSpec((n_seq, t_chunk, LANES), lambda j, i: (0, i, j)),
                   pl.BlockSpec((1, SUBLANES, 2 * S5_STATE_COLS), lambda j, i: (j, 0, 0))),
        scratch_shapes=[pltpu.VMEM((r_chunk, LANES), F32),
                        pltpu.VMEM((r_chunk, 2 * S5_STATE_COLS), F32),
                        pltpu.VMEM((SUBLANES, 2 * S5_STATE_COLS), F32)],
        compiler_params=_params(("parallel", "arbitrary")),
        name="s5_scan",
    )(z3, bb, cc, d_row, a_re, a_im, h0)


def _s5_post_kernel(y_ref, w_ref, g_ref, o_ref):
    y = y_ref[...]
    y = 0.5 * y * (1.0 + jnp.tanh(math.sqrt(2.0 / math.pi) * (y + 0.044715 * (y * y * y))))
    gate = _sigmoid(jnp.dot(y.astype(BF16), w_ref[...], preferred_element_type=F32))
    y = y * gate
    ms = jnp.mean(y * y, axis=-1, keepdims=True)
    o_ref[...] = (y * lax.rsqrt(ms + EPS) * g_ref[...]).astype(o_ref.dtype)


def _s5_post(y, w_glu, g):
    m, w = y.shape
    tm = _pick_tile(m, (512, 256, 128, 64))
    return pl.pallas_call(
        _s5_post_kernel,
        out_shape=jax.ShapeDtypeStruct((m, w), BF16),
        grid=(m // tm,),
        in_specs=[pl.BlockSpec((tm, w), lambda i: (i, 0)), pl.BlockSpec((w, w), lambda i: (0, 0)),
                  pl.BlockSpec((1, w), lambda i: (0, 0))],
        out_specs=pl.BlockSpec((tm, w), lambda i: (i, 0)),
        compiler_params=_params(("parallel",)),
        name="s5_gelu_glu_norm",
    )(y, w_glu, g.reshape(1, w))


def _s5_discretise(lam_re, lam_im, log_dt, b_re, b_im, c_re, c_im):
    g_cnt = lam_re.shape[0]
    dt = jnp.exp(log_dt.astype(F32))[:, None]
    mag = jnp.exp(lam_re * dt)
    ang = lam_im * dt
    a_re, a_im = mag * jnp.cos(ang), mag * jnp.sin(ang)
    den = lam_re * lam_re + lam_im * lam_im
    num_re = a_re - 1.0
    coef_re = (num_re * lam_re + a_im * lam_im) / den
    coef_im = (a_im * lam_re - num_re * lam_im) / den
    bb_re = coef_re[..., None] * b_re - coef_im[..., None] * b_im
    bb_im = coef_re[..., None] * b_im + coef_im[..., None] * b_re
    nb = g_cnt // S5_GROUPS_PER_BLOCK
    gb = S5_GROUPS_PER_BLOCK
    eye = jnp.eye(gb, dtype=F32)

    def block_diag_in(w):
        w = w.reshape(nb, gb, SSM_P, SSM_H)
        return jnp.einsum('ngph,gk->nghkp', w, eye).reshape(nb, gb * SSM_H, gb * SSM_P)

    def block_diag_out(w):
        w = w.reshape(nb, gb, SSM_H, SSM_P)
        return jnp.einsum('nghp,gk->ngpkh', w, eye).reshape(nb, gb * SSM_P, gb * SSM_H)

    bb = jnp.concatenate([block_diag_in(bb_re), block_diag_in(bb_im)], axis=2).astype(BF16)
    cc = jnp.concatenate([block_diag_out(c_re.astype(F32)), block_diag_out(-c_im.astype(F32))], axis=1).astype(BF16)
    a_re_b = a_re.reshape(nb, 1, gb * SSM_P)
    a_im_b = a_im.reshape(nb, 1, gb * SSM_P)
    return bb, cc, a_re_b, a_im_b


def _s5_mixer(z, b, l, col0, w_ssm, prm, h0_re, h0_im):
    nb = w_ssm // LANES
    gb = S5_GROUPS_PER_BLOCK
    def pack_state(h):
        h = jnp.pad(h.astype(F32), ((0, SUBLANES - b), (0, 0), (0, 0)))
        return h.reshape(SUBLANES, nb, gb * SSM_P).transpose(1, 0, 2)

    h0 = jnp.concatenate([pack_state(h0_re), pack_state(h0_im)], axis=2)
    y, xfin = _s5(z.reshape(b, l, z.shape[1]), col0 // LANES, w_ssm,
                  prm['bb'], prm['cc'], prm['d'], prm['a_re'], prm['a_im'], h0)
    o = _s5_post(y.reshape(b * l, w_ssm), prm['w_glu'], prm['norm_g'])

    def unpack_state(x):
        return x.transpose(1, 0, 2).reshape(SUBLANES, nb * gb, SSM_P)[:b]

    return o, unpack_state(xfin[:, :, :S5_STATE_COLS]), unpack_state(xfin[:, :, S5_STATE_COLS:])


def _att_rope_tables(pos):
    freqs = ROPE_THETA ** (-jnp.arange(0, DK_HALF, 2, dtype=F32) / DK_HALF)
    ang = pos.astype(F32)[:, None] * freqs[None, :]
    c, s = jnp.cos(ang), jnp.sin(ang)
    zero = jnp.zeros_like(s)
    return (jnp.concatenate([c, c, c, c], axis=-1),
            jnp.concatenate([-s, zero, -s, zero], axis=-1),
            jnp.concatenate([zero, s, zero, s], axis=-1))


def _ret_rope_tables(pos):
    freqs = 1.0 / (ROPE_THETA ** jnp.linspace(0.0, 1.0, HEAD_DIM // 2, dtype=F32))
    ang = pos.astype(F32)[:, None] * freqs[None, :]
    c, s = jnp.cos(ang), jnp.sin(ang)
    return jnp.concatenate([c, c], axis=-1), jnp.concatenate([-s, s], axis=-1)


def _row128(v):
    return jnp.tile(v.astype(F32), HEAD_DIM // v.shape[0]).reshape(1, HEAD_DIM)


def _mixers(z, b, l, lw, attend, ret_tabs, ret_s0, s5_h0_re, s5_h0_im):
    d = lw['d_model']
    w_att = lw['w_att']
    o_att, new_k, new_v = attend(z)
    o_ret, s_ret = _retention(z, b, l, 3 * w_att // HEAD_DIM, w_att // HEAD_DIM, ret_tabs, lw['log_gamma'], ret_s0)
    o_ssm, s5_re, s5_im = _s5_mixer(z, b, l, 7 * w_att, d - 2 * w_att, lw['s5'], s5_h0_re, s5_h0_im)
    return [o_att, o_ret.astype(BF16), o_ssm], (new_k, new_v, s_ret, s5_re, s5_im)


def kernel(x_prompt, x_sample, cache_k, cache_v, page_table, state_ret, state_s5_re, state_s5_im,
           norm_mix_g, w_in, q_norm_g, k_norm_g, lambda_q1, lambda_k1, lambda_q2, lambda_k2, subln_g,
           s5_lambda_re, s5_lambda_im, s5_log_dt, s5_b_re, s5_b_im, s5_c_re, s5_c_im, s5_d, s5_w_glu,
           s5_norm_g, w_out, norm_ffn_g, w_gate, w_up, w_down):
    b_p, seq, d = x_prompt.shape
    b_s, n_new, _ = x_sample.shape
    depth = w_in.shape[0]
    w_att = (3 * d) // (8 * HEAD_DIM) * HEAD_DIM
    h_att = w_att // HEAD_DIM
    w_ssm = d - 2 * w_att
    n_pages = page_table.shape[1]
    past = n_pages * PAGE_SIZE
    assert (2 * w_att) % w_ssm == 0 and w_ssm % LANES == 0 and b_p <= SUBLANES and b_s <= SUBLANES
    ssm_rb = (2 * w_att) // w_ssm

    pos_p = jnp.arange(seq, dtype=jnp.int32)
    pos_s = past + jnp.arange(n_new, dtype=jnp.int32)
    att_tabs_p, att_tabs_s = _att_rope_tables(pos_p), _att_rope_tables(pos_s)
    ret_tabs_p, ret_tabs_s = _ret_rope_tables(pos_p), _ret_rope_tables(pos_s)
    log_gamma = jnp.log(1.0 - 2.0 ** (-5.0 - jnp.arange(h_att, dtype=F32)))
    log_gamma_rows = jnp.broadcast_to(log_gamma[:, None, None], (h_att, 1, HEAD_DIM))

    xp = x_prompt.reshape(b_p * seq, d)
    xs = x_sample.reshape(b_s * n_new, d)
    zeros_ret = jnp.zeros((b_p, h_att, HEAD_DIM, HEAD_DIM), F32)
    zeros_s5 = jnp.zeros((b_p, w_ssm // SSM_H, SSM_P), F32)
    w_down_b = w_down.astype(BF16)
    cache_kh = jnp.transpose(cache_k, (0, 1, 3, 2, 4))
    cache_vh = jnp.transpose(cache_v, (0, 1, 3, 2, 4))
    outs = [[] for _ in range(8)]
    kv_prompt = None
    for layer in range(depth):
        lam_init = 0.8 - 0.6 * math.exp(-0.3 * layer)
        lam = (jnp.exp(jnp.sum(lambda_q1[layer].astype(F32) * lambda_k1[layer].astype(F32)))
               - jnp.exp(jnp.sum(lambda_q2[layer].astype(F32) * lambda_k2[layer].astype(F32))) + lam_init)
        lam_row = jnp.broadcast_to(lam.reshape(1, 1), (1, HEAD_DIM)).astype(F32)
        bb, cc, a_re, a_im = _s5_discretise(
            s5_lambda_re[layer].astype(F32), s5_lambda_im[layer].astype(F32), s5_log_dt[layer],
            s5_b_re[layer].astype(F32), s5_b_im[layer].astype(F32), s5_c_re[layer], s5_c_im[layer])
        lw = {
            'd_model': d, 'w_att': w_att, 'log_gamma': log_gamma_rows,
            's5': {'bb': bb, 'cc': cc, 'a_re': a_re, 'a_im': a_im,
                   'd': s5_d[layer].astype(F32).reshape(1, w_ssm),
                   'w_glu': s5_w_glu[layer].astype(BF16), 'norm_g': s5_norm_g[layer].astype(F32)},
        }
        qg, kg, sg = _row128(q_norm_g[layer]), _row128(k_norm_g[layer]), _row128(subln_g[layer])
        out_scale = 1.0 - lam_init

        hp, hs = _rmsnorm(xp, norm_mix_g[layer]), _rmsnorm(xs, norm_mix_g[layer])
        zp, zs = _matmul_pair([hp], [hs], w_in, layer, [0], tm=1024, tn=512, name="in_proj")

        attend_p = functools.partial(_prompt_attention, b=b_p, l=seq, w_att=w_att, tabs=att_tabs_p,
                                     qg=qg, kg=kg, sg=sg, lam_row=lam_row, out_scale=out_scale,
                                     layer=layer, depth=depth, kv_prev=kv_prompt)
        mix_p, (kp, vp, rp, p_re, p_im) = _mixers(zp, b_p, seq, lw, attend_p, ret_tabs_p,
                                                  zeros_ret, zeros_s5, zeros_s5)
        kv_prompt = (kp, vp)

        def attend_s(z):
            o, nk, nv = _sample_attention(z.reshape(b_s, n_new, z.shape[1]), cache_kh, cache_vh, layer,
                                          page_table, w_att, att_tabs_s, qg, kg, sg, lam_row, out_scale)
            flat = lambda t: t.reshape(b_s * n_new, w_att)
            return flat(o).astype(BF16), flat(nk), flat(nv)

        mix_s, (k_s, v_s, r_s, s_re, s_im) = _mixers(zs, b_s, n_new, lw, attend_s, ret_tabs_s, state_ret[layer],
                                                     state_s5_re[layer], state_s5_im[layer])

        xp, xs = _matmul_pair(mix_p, mix_s, w_out, layer, [0, 1, ssm_rb], mode="res", res_p=xp, res_s=xs,
                              tm=1024, tn=512, name="out_proj")
        hp, hs = _rmsnorm(xp, norm_ffn_g[layer]), _rmsnorm(xs, norm_ffn_g[layer])
        act_p, act_s = _matmul_pair([hp], [hs], w_gate, layer, [0], mode="swiglu", w2=w_up,
                                    out_dtype=BF16, tm=1024, tn=256, name="ffn_gate_up")
        xp = _matmul_res(act_p, w_down_b, layer, xp, tm=512, tn=512, name="ffn_down")
        xs = _matmul_res(act_s, w_down_b, layer, xs, tm=512, tn=512, name="ffn_down")

        for lst, val in zip(outs, (k_s.reshape(b_s, n_new, h_att, HEAD_DIM), v_s.reshape(b_s, n_new, h_att, HEAD_DIM),
                                   rp, r_s, p_re, p_im, s_re, s_im)):
            lst.append(val)
    kv_out = tuple(jnp.transpose(t, (0, 1, 3, 2, 4)) for t in kv_prompt)
    return (xp.reshape(b_p, seq, d), xs.reshape(b_s, n_new, d)) + kv_out + tuple(jnp.stack(o) for o in outs)
```
